```python
import math
import jax
import jax.numpy as jnp
from jax import lax
import numpy as np

D_MODEL = 1024
BATCH = 16
SEQ = 2048
DEPTH = 2

MIX_DIM = D_MODEL
HEAD_DIM = 64
RWKV_DIM = MIX_DIM // 2
RWKV_HEADS = RWKV_DIM // HEAD_DIM
RWKV_DECAY_LORA = 64
RWKV_A_LORA = 64
RWKV_GATE_LORA = 128
RWKV_GN_EPS = 64e-5
RWKV_COLS = 3 * RWKV_DIM + RWKV_DECAY_LORA + RWKV_A_LORA + RWKV_GATE_LORA
SSM_DIM = MIX_DIM // 2
SSM_HEAD_DIM = 64
SSM_HEADS = SSM_DIM // SSM_HEAD_DIM
SSM_GROUPS = 2
SSM_STATE = 128
SSM_CONV = 4
SSM_CHUNK = 128
SSM_XBC = SSM_DIM + 2 * SSM_GROUPS * SSM_STATE
SSM_COLS = SSM_DIM + SSM_XBC + SSM_HEADS
L0_COLS = RWKV_COLS + SSM_COLS
SB_DIM = MIX_DIM // 2
SB_HEADS = SB_DIM // HEAD_DIM
MLA_NOPE = 64
MLA_ROPE = 32
MLA_V = 64
MLA_HEADS = (MIX_DIM // 2) // MLA_V
MLA_Q_LORA = 256
MLA_KV_LORA = 128
ROPE_THETA = 10000.0
L1_COLS = 3 * SB_DIM + MLA_Q_LORA + MLA_KV_LORA + MLA_ROPE
Q_BLOCK = 128
D_FF = 2816
FFN_CONV = 3
ALPHA = (2 * DEPTH) ** 0.25
BETA = (8 * DEPTH) ** -0.25

kernel_name = 'hybrid_rwkv7_ssd_stickbreak_mla_convffn'


def _layer_norm(x, g, b, eps=1e-5):
    xf = x.astype(jnp.float32)
    mu = jnp.mean(xf, axis=-1, keepdims=True)
    var = jnp.mean(jnp.square(xf - mu), axis=-1, keepdims=True)
    return ((xf - mu) * lax.rsqrt(var + eps) * g + b).astype(x.dtype)


def _rms_norm(x, g, eps=1e-6):
    xf = x.astype(jnp.float32)
    return (xf * lax.rsqrt(jnp.mean(xf * xf, axis=-1, keepdims=True) + eps) * g).astype(x.dtype)


def _causal_dwconv(u, w, b):
    K = w.shape[0]
    T = u.shape[1]
    up = jnp.pad(u, ((0, 0), (K - 1, 0), (0, 0)))
    y = b + up[:, 0:T] * w[0]
    for i in range(1, K):
        y = y + up[:, i:i + T] * w[i]
    return y


def _to_heads(t, n):
    return t.reshape(t.shape[0], t.shape[1], n, -1)


def _rwkv7_scan(r, w, k, v, a, b):
    bsz, T, H, N = r.shape

    def step(S, inp):
        r_t, w_t, k_t, v_t, a_t, b_t = inp
        sa = jnp.einsum('bhij,bhj->bhi', S, a_t)
        S = S * w_t[:, :, None, :] + sa[..., None] * b_t[:, :, None, :] + v_t[..., None] * k_t[:, :, None, :]
        return S, jnp.einsum('bhij,bhj->bhi', S, r_t)

    xs = tuple(jnp.swapaxes(t, 0, 1) for t in (r, w, k, v, a, b))
    _, y = lax.scan(step, jnp.zeros((bsz, H, N, N), jnp.float32), xs)
    return jnp.swapaxes(y, 0, 1)


def _rwkv7_group(p, mix, w0, w2, a0, a2, g2, k_k, k_a, r_k, ln_g, ln_b):
    bsz, T, _ = p.shape
    prev = jnp.pad(p, ((0, 0), (1, 0), (0, 0)))[:, :-1]
    p = p + (prev - p) * mix
    cuts = [RWKV_DIM, 2 * RWKV_DIM, 3 * RWKV_DIM, 3 * RWKV_DIM + RWKV_DECAY_LORA,
            3 * RWKV_DIM + RWKV_DECAY_LORA + RWKV_A_LORA]
    r, k, v, w_lo, a_lo, g_lo = jnp.split(p, cuts, axis=-1)
    log_w = -jax.nn.softplus(-(w0 + jnp.tanh(w_lo) @ w2)) - 0.5
    decay = jnp.exp(-jnp.exp(log_w.astype(jnp.float32)))
    a = jax.nn.sigmoid(a0 + a_lo @ a2)
    g = jax.nn.sigmoid(g_lo) @ g2
    kk = _to_heads(k * k_k, RWKV_HEADS).astype(jnp.float32)
    kk = kk / jnp.maximum(jnp.sqrt(jnp.sum(kk * kk, axis=-1, keepdims=True)), 1e-12)
    k = k * (1 + (a - 1) * k_a)
    r_h, k_h, v_h, a_h, w_h = [_to_heads(t, RWKV_HEADS).astype(jnp.float32) for t in (r, k, v, a, decay)]
    y = _rwkv7_scan(r_h, w_h, k_h, v_h, -kk, kk * a_h)
    mu = jnp.mean(y, axis=-1, keepdims=True)
    var = jnp.mean(jnp.square(y - mu), axis=-1, keepdims=True)
    y = ((y - mu) * lax.rsqrt(var + RWKV_GN_EPS)).reshape(bsz, T, RWKV_DIM) * ln_g + ln_b
    bonus = jnp.sum(r_h * k_h * r_k, axis=-1, keepdims=True) * v_h
    return ((y + bonus.reshape(bsz, T, RWKV_DIM)) * g).astype(p.dtype)


def _segsum(a):
    L = a.shape[-1]
    ar = jnp.broadcast_to(a[..., :, None], a.shape + (L,))
    strict = jnp.tril(jnp.ones((L, L), bool), -1)
    s = jnp.cumsum(jnp.where(strict, ar, 0.0), axis=-2)
    return jnp.where(jnp.tril(jnp.ones((L, L), bool)), s, -jnp.inf)


def _ssd_chunked(xs, dt, A, Bm, Cm):
    bsz, T, H, P = xs.shape
    G, N = Bm.shape[2], Bm.shape[3]
    J = H // G
    c, l = T // SSM_CHUNK, SSM_CHUNK
    X = (xs * dt[..., None]).reshape(bsz, c, l, G, J, P)
    a_dt = (dt * A).reshape(bsz, c, l, G, J).transpose(0, 3, 4, 1, 2)
    Bc = Bm.reshape(bsz, c, l, G, N)
    Cc = Cm.reshape(bsz, c, l, G, N)
    a_cum = jnp.cumsum(a_dt, axis=-1)
    decay_in = jnp.exp(_segsum(a_dt))
    cb = jnp.einsum('bclgn,bcsgn->bgcls', Cc, Bc)
    y_diag = jnp.einsum('bgjcls,bcsgjp->bclgjp', cb[:, :, None] * decay_in, X)
    decay_to_end = jnp.exp(a_cum[..., -1:] - a_cum)
    states = jnp.einsum('bclgn,bgjcl,bclgjp->bcgjpn', Bc, decay_to_end, X)
    states = jnp.concatenate([jnp.zeros_like(states[:, :1]), states], axis=1)
    chunk_decay = jnp.exp(_segsum(jnp.pad(a_cum[..., -1], ((0, 0), (0, 0), (0, 0), (1, 0)))))
    states = jnp.einsum('bgjzc,bcgjpn->bzgjpn', chunk_decay, states)[:, :-1]
    y_off = jnp.einsum('bclgn,bcgjpn,bgjcl->bclgjp', Cc, states, jnp.exp(a_cum))
    return (y_diag + y_off).reshape(bsz, T, H, P)


def _mamba2_group(p, conv_w, conv_b, dt_bias, a_log, d_skip, norm_g):
    bsz, T, _ = p.shape
    z, xbc, dt_raw = jnp.split(p, [SSM_DIM, SSM_DIM + SSM_XBC], axis=-1)
    xbc = jax.nn.silu(_causal_dwconv(xbc, conv_w, conv_b))
    xs, Bm, Cm = jnp.split(xbc, [SSM_DIM, SSM_DIM + SSM_GROUPS * SSM_STATE], axis=-1)
    xs = xs.reshape(bsz, T, SSM_HEADS, SSM_HEAD_DIM).astype(jnp.float32)
    Bm = Bm.reshape(bsz, T, SSM_GROUPS, SSM_STATE).astype(jnp.float32)
    Cm = Cm.reshape(bsz, T, SSM_GROUPS, SSM_STATE).astype(jnp.float32)
    dt = jax.nn.softplus((dt_raw + dt_bias).astype(jnp.float32))
    A = -jnp.exp(a_log.astype(jnp.float32))
    y = _ssd_chunked(xs, dt, A, Bm, Cm) + xs * d_skip[:, None]
    u = (y.reshape(bsz, T, SSM_DIM) * jax.nn.silu(z.astype(jnp.float32))).reshape(bsz, T, SSM_GROUPS, -1)
    u = u * lax.rsqrt(jnp.mean(u * u, axis=-1, keepdims=True) + 1e-5)
    return (u.reshape(bsz, T, SSM_DIM) * norm_g).astype(p.dtype)


def _mixer_rwkv_ssd(h, w_in, mix, w0, w2, a0, a2, g2, k_k, k_a, r_k, ln_g, ln_b,
                    conv_w, conv_b, dt_bias, a_log, d_skip, norm_g, w_out):
    proj = h @ w_in
    y_a = _rwkv7_group(proj[..., :RWKV_COLS], mix, w0, w2, a0, a2, g2, k_k, k_a, r_k, ln_g, ln_b)
    y_b = _mamba2_group(proj[..., RWKV_COLS:], conv_w, conv_b, dt_bias, a_log, d_skip, norm_g)
    return jnp.concatenate([y_a, y_b], axis=-1) @ w_out


def _stick_breaking(q, k, v):
    T = q.shape[2]
    scale = q.shape[-1] ** -0.5
    outs = []
    for start in range(0, T, Q_BLOCK):
        end = start + Q_BLOCK
        z = jnp.einsum('bhqd,bhkd->bhqk', q[:, :, start:end], k[:, :, :end]).astype(jnp.float32) * scale
        strict = jnp.arange(end)[None, :] < jnp.arange(start, end)[:, None]
        log_keep = jnp.where(strict, jax.nn.log_sigmoid(-z), 0.0)
        log_att = jax.nn.log_sigmoid(z) + lax.cumsum(log_keep, axis=3, reverse=True) - log_keep
        att = jnp.where(strict, jnp.exp(log_att), 0.0)
        outs.append(jnp.einsum('bhqk,bhkd->bhqd', att.astype(v.dtype), v[:, :, :end]))
    return jnp.concatenate(outs, axis=2)


def _rope_tables(positions):
    inv_freq = 1.0 / (ROPE_THETA ** (jnp.arange(0, MLA_ROPE, 2, dtype=jnp.float32) / MLA_ROPE))
    ang = positions.astype(jnp.float32)[..., None] * inv_freq
    return jnp.cos(ang), jnp.sin(ang)


def _apply_rope(x, cos, sin):
    half = x.shape[-1] // 2
    x1, x2 = x[..., :half], x[..., half:]
    return jnp.concatenate([x1 * cos - x2 * sin, x2 * cos + x1 * sin], axis=-1)


def _mla_attention(q_nope, q_pe, k_nope, k_pe, v):
    T = q_nope.shape[2]
    scale = (MLA_NOPE + MLA_ROPE) ** -0.5
    outs = []
    for start in range(0, T, Q_BLOCK):
        end = start + Q_BLOCK
        s = (jnp.einsum('bhqd,bhkd->bhqk', q_nope[:, :, start:end], k_nope[:, :, :end])
             + jnp.einsum('bhqd,bkd->bhqk', q_pe[:, :, start:end], k_pe[:, :end])).astype(jnp.float32) * scale
        causal = jnp.arange(end)[None, :] <= jnp.arange(start, end)[:, None]
        prob = jax.nn.softmax(jnp.where(causal, s, -jnp.inf), axis=-1)
        outs.append(jnp.einsum('bhqk,bhkd->bhqd', prob.astype(v.dtype), v[:, :, :end]))
    return jnp.concatenate(outs, axis=2)


def _mixer_sb_mla(h, positions, w_in, q_norm_g, w_uq, kv_norm_g, w_ukv, w_out):
    bsz, T, _ = h.shape
    proj = h @ w_in
    cuts = [SB_DIM, 2 * SB_DIM, 3 * SB_DIM, 3 * SB_DIM + MLA_Q_LORA, 3 * SB_DIM + MLA_Q_LORA + MLA_KV_LORA]
    q_sb, k_sb, v_sb, c_q, c_kv, k_pe = jnp.split(proj, cuts, axis=-1)
    tr = lambda t: t.transpose(0, 2, 1, 3)
    y_c = _stick_breaking(tr(_to_heads(q_sb, SB_HEADS)), tr(_to_heads(k_sb, SB_HEADS)), tr(_to_heads(v_sb, SB_HEADS)))
    y_c = tr(y_c).reshape(bsz, T, SB_DIM)
    q = _to_heads(_rms_norm(c_q, q_norm_g) @ w_uq, MLA_HEADS)
    kv = _to_heads(_rms_norm(c_kv, kv_norm_g) @ w_ukv, MLA_HEADS)
    cos, sin = _rope_tables(positions)
    q_pe = _apply_rope(q[..., MLA_NOPE:], cos[:, :, None], sin[:, :, None])
    k_pe = _apply_rope(k_pe, cos, sin)
    y_d = _mla_attention(tr(q[..., :MLA_NOPE]), tr(q_pe), tr(kv[..., :MLA_NOPE]), k_pe, tr(kv[..., MLA_NOPE:]))
    y_d = tr(y_d).reshape(bsz, T, MLA_HEADS * MLA_V)
    return jnp.concatenate([y_c, y_d.astype(y_c.dtype)], axis=-1) @ w_out


def _conv_ffn(h, w_up, conv_w, conv_b, w_down):
    gate, up = jnp.split(h @ w_up, [D_FF], axis=-1)
    gate = _causal_dwconv(gate, conv_w, conv_b)
    return (jax.nn.silu(gate) * up) @ w_down


def setup_inputs(seed: int = 0) -> dict:
    key = jax.random.key(seed)
    ks = iter(jax.random.split(key, 64))

    def nrm(shape, scale):
        return jax.random.normal(next(ks), shape, jnp.float32) * scale

    def uni(shape, lo, hi):
        return jax.random.uniform(next(ks), shape, jnp.float32, lo, hi)

    def gain(n):
        return 1.0 + nrm((n,), 0.02)

    inp = {}
    inp['x'] = nrm((BATCH, SEQ, D_MODEL), 1.0)
    inp['positions'] = (jax.random.randint(next(ks), (BATCH, 1), 0, 4096, dtype=jnp.int32)
                        + jnp.arange(SEQ, dtype=jnp.int32)[None, :])
    inp['l0_w_in'] = nrm((D_MODEL, L0_COLS), D_MODEL ** -0.5)
    inp['rwkv_mix'] = uni((RWKV_COLS,), 0.0, 1.0)
    inp['rwkv_w0'] = uni((RWKV_DIM,), -6.0, -1.0)
    inp['rwkv_w2'] = nrm((RWKV_DECAY_LORA, RWKV_DIM), 0.1)
    inp['rwkv_a0'] = nrm((RWKV_DIM,), 0.1)
    inp['rwkv_a2'] = nrm((RWKV_A_LORA, RWKV_DIM), 0.1)
    inp['rwkv_g2'] = nrm((RWKV_GATE_LORA, RWKV_DIM), RWKV_GATE_LORA ** -0.5)
    inp['rwkv_k_k'] = 0.85 + nrm((RWKV_DIM,), 0.05)
    inp['rwkv_k_a'] = 1.0 + nrm((RWKV_DIM,), 0.05)
    inp['rwkv_r_k'] = nrm((RWKV_HEADS, HEAD_DIM), 0.1)
    inp['rwkv_ln_g'] = gain(RWKV_DIM)
    inp['rwkv_ln_b'] = nrm((RWKV_DIM,), 0.02)
    inp['ssm_conv_w'] = nrm((SSM_CONV, SSM_XBC), 0.5)
    inp['ssm_conv_b'] = nrm((SSM_XBC,), 0.02)
    dt0 = jnp.exp(uni((SSM_HEADS,), math.log(1e-3), math.log(1e-1)))
    inp['ssm_dt_bias'] = dt0 + jnp.log(-jnp.expm1(-dt0))
    inp['ssm_a_log'] = jnp.log(uni((SSM_HEADS,), 1.0, 16.0))
    inp['ssm_d'] = 1.0 + nrm((SSM_HEADS,), 0.1)
    inp['ssm_norm_g'] = gain(SSM_DIM)
    inp['l0_w_out'] = nrm((MIX_DIM, D_MODEL), MIX_DIM ** -0.5 * BETA)
    inp['l0_ln1_g'] = gain(D_MODEL)
    inp['l0_ln1_b'] = nrm((D_MODEL,), 0.02)
    inp['ffn0_w_up'] = nrm((D_MODEL, 2 * D_FF), D_MODEL ** -0.5)
    inp['ffn0_conv_w'] = nrm((FFN_CONV, D_FF), FFN_CONV ** -0.5)
    inp['ffn0_conv_b'] = nrm((D_FF,), 0.02)
    inp['ffn0_w_down'] = nrm((D_FF, D_MODEL), D_FF ** -0.5 * BETA)
    inp['l0_ln2_g'] = gain(D_MODEL)
    inp['l0_ln2_b'] = nrm((D_MODEL,), 0.02)
    inp['l1_w_in'] = nrm((D_MODEL, L1_COLS), D_MODEL ** -0.5)
    inp['mla_q_norm_g'] = gain(MLA_Q_LORA)
    inp['mla_w_uq'] = nrm((MLA_Q_LORA, MLA_HEADS * (MLA_NOPE + MLA_ROPE)), MLA_Q_LORA ** -0.5)
    inp['mla_kv_norm_g'] = gain(MLA_KV_LORA)
    inp['mla_w_ukv'] = nrm((MLA_KV_LORA, MLA_HEADS * (MLA_NOPE + MLA_V)), MLA_KV_LORA ** -0.5)
    inp['l1_w_out'] = nrm((MIX_DIM, D_MODEL), MIX_DIM ** -0.5 * BETA)
    inp['l1_ln1_g'] = gain(D_MODEL)
    inp['l1_ln1_b'] = nrm((D_MODEL,), 0.02)
    inp['ffn1_w_up'] = nrm((D_MODEL, 2 * D_FF), D_MODEL ** -0.5)
    inp['ffn1_conv_w'] = nrm((FFN_CONV, D_FF), FFN_CONV ** -0.5)
    inp['ffn1_conv_b'] = nrm((D_FF,), 0.02)
    inp['ffn1_w_down'] = nrm((D_FF, D_MODEL), D_FF ** -0.5 * BETA)
    inp['l1_ln2_g'] = gain(D_MODEL)
    inp['l1_ln2_b'] = nrm((D_MODEL,), 0.02)
    return inp


def reference(x, positions, l0_w_in, rwkv_mix, rwkv_w0, rwkv_w2, rwkv_a0, rwkv_a2, rwkv_g2,
              rwkv_k_k, rwkv_k_a, rwkv_r_k, rwkv_ln_g, rwkv_ln_b, ssm_conv_w, ssm_conv_b,
              ssm_dt_bias, ssm_a_log, ssm_d, ssm_norm_g, l0_w_out, l0_ln1_g, l0_ln1_b,
              ffn0_w_up, ffn0_conv_w, ffn0_conv_b, ffn0_w_down, l0_ln2_g, l0_ln2_b,
              l1_w_in, mla_q_norm_g, mla_w_uq, mla_kv_norm_g, mla_w_ukv, l1_w_out,
              l1_ln1_g, l1_ln1_b, ffn1_w_up, ffn1_conv_w, ffn1_conv_b, ffn1_w_down,
              l1_ln2_g, l1_ln2_b):
    mixers = (_mixer_rwkv_ssd, _mixer_sb_mla)
    mixer_args = (
        (l0_w_in, rwkv_mix, rwkv_w0, rwkv_w2, rwkv_a0, rwkv_a2, rwkv_g2, rwkv_k_k, rwkv_k_a,
         rwkv_r_k, rwkv_ln_g, rwkv_ln_b, ssm_conv_w, ssm_conv_b, ssm_dt_bias, ssm_a_log,
         ssm_d, ssm_norm_g, l0_w_out),
        (positions, l1_w_in, mla_q_norm_g, mla_w_uq, mla_kv_norm_g, mla_w_ukv, l1_w_out),
    )
    ffn_args = ((ffn0_w_up, ffn0_conv_w, ffn0_conv_b, ffn0_w_down),
                (ffn1_w_up, ffn1_conv_w, ffn1_conv_b, ffn1_w_down))
    ln_mix = ((l0_ln1_g, l0_ln1_b), (l1_ln1_g, l1_ln1_b))
    ln_ffn = ((l0_ln2_g, l0_ln2_b), (l1_ln2_g, l1_ln2_b))
    h = x
    for layer in range(DEPTH):
        mixed = mixers[layer % 2](h, *mixer_args[layer])
        h = _layer_norm(ALPHA * h + mixed, *ln_mix[layer])
        h = _layer_norm(ALPHA * h + _conv_ffn(h, *ffn_args[layer]), *ln_ffn[layer])
    return h.astype(x.dtype)
```

```python
import functools
import math

import jax
import jax.numpy as jnp
from jax import lax
from jax.experimental import pallas as pl
from jax.experimental.pallas import tpu as pltpu

F32 = jnp.float32
BF16 = jnp.bfloat16

D_MODEL = 1024
HEAD_DIM = 64
N_HEADS = 8
HALF = N_HEADS * HEAD_DIM
RWKV_DECAY_LORA = 64
RWKV_A_LORA = 64
RWKV_GATE_LORA = 128
RWKV_GN_EPS = 64e-5
RWKV_COLS = 3 * HALF + RWKV_DECAY_LORA + RWKV_A_LORA + RWKV_GATE_LORA
SSM_GROUPS = 2
SSM_STATE = 128
SSM_CONV = 4
SSM_CHUNK = 128
SSM_XBC = HALF + 2 * SSM_GROUPS * SSM_STATE
SSM_COLS = HALF + SSM_XBC + N_HEADS
SSM_COLS_PAD = SSM_XBC + HALF + 128
MLA_NOPE = 64
MLA_ROPE = 32
MLA_V = 64
MLA_Q_LORA = 256
MLA_KV_LORA = 128
ROPE_THETA = 10000.0
D_FF = 2816
FFN_CONV = 3
DEPTH = 2
ALPHA = (2 * DEPTH) ** 0.25
LN_EPS = 1e-5

LANES = 128
SUBLANES = 8
VMEM_LIMIT = 56 * 1024 * 1024
NEG_BIG = -1e30

_HI = lax.Precision.HIGHEST


def _params(*sem):
    return pltpu.CompilerParams(dimension_semantics=sem, vmem_limit_bytes=VMEM_LIMIT)


def _dot(a, b, precision=None):
    return jnp.dot(a, b, preferred_element_type=F32, precision=precision)


def _dot_nt(a, b):
    return lax.dot_general(a, b, (((1,), (1,)), ((), ())), preferred_element_type=F32)


def _sigmoid(x):
    return 1.0 / (1.0 + jnp.exp(-x))


def _softplus(x):
    return jnp.maximum(x, 0.0) + jnp.log1p(jnp.exp(-jnp.abs(x)))


def _silu(x):
    return x * _sigmoid(x)


def _layer_norm(v, g, b):
    mu = jnp.mean(v, axis=-1, keepdims=True)
    d = v - mu
    var = jnp.mean(d * d, axis=-1, keepdims=True)
    return d * lax.rsqrt(var + LN_EPS) * g + b


def _const_spec(shape):
    nd = len(shape)
    return pl.BlockSpec(shape, lambda *_: (0,) * nd)


def _rwkv_in_body(x_ref, w_ref, mix_ref, wl_ref, w0a0_ref, g2_ref, o_ref, g_ref, carry_ref, *, tiles_per_seq):
    i = pl.program_id(0)
    p = _dot(x_ref[...].astype(BF16), w_ref[...])
    tm = p.shape[0]
    first = (i % tiles_per_seq) == 0
    prev_row = jnp.where(first, 0.0, carry_ref[SUBLANES - 1:SUBLANES, :])
    row = lax.broadcasted_iota(jnp.int32, p.shape, 0)
    prev = jnp.where(row == 0, prev_row, pltpu.roll(p, 1, 0))
    carry_ref[...] = p[tm - SUBLANES:tm, :]
    pm = p + (prev - p) * mix_ref[...]
    lo = pm[:, 3 * HALF:3 * HALF + LANES]
    lane = lax.broadcasted_iota(jnp.int32, lo.shape, 1)
    lo = jnp.where(lane < RWKV_DECAY_LORA, jnp.tanh(lo), lo)
    wa = _dot(lo.astype(BF16), wl_ref[...]) + w0a0_ref[...]
    log_w = -_softplus(-wa[:, :HALF]) - 0.5
    o_ref[:, 0:3 * HALF] = pm[:, 0:3 * HALF]
    o_ref[:, 3 * HALF:4 * HALF] = jnp.exp(-jnp.exp(log_w))
    o_ref[:, 4 * HALF:5 * HALF] = _sigmoid(wa[:, HALF:])
    g_lo = _sigmoid(pm[:, 3 * HALF + LANES:])
    g_ref[...] = _dot(g_lo.astype(BF16), g2_ref[...])


def _rwkv_in(x2, w_r, mix, wl, w0a0, g2, *, seq, tm=256):
    m = x2.shape[0]
    body = functools.partial(_rwkv_in_body, tiles_per_seq=seq // tm)
    return pl.pallas_call(
        body,
        grid=(m // tm,),
        in_specs=[
            pl.BlockSpec((tm, D_MODEL), lambda i: (i, 0)),
            _const_spec(w_r.shape), _const_spec(mix.shape), _const_spec(wl.shape),
            _const_spec(w0a0.shape), _const_spec(g2.shape),
        ],
        out_specs=[
            pl.BlockSpec((tm, 5 * HALF), lambda i: (i, 0)),
            pl.BlockSpec((tm, HALF), lambda i: (i, 0)),
        ],
        out_shape=[jax.ShapeDtypeStruct((m, 5 * HALF), F32), jax.ShapeDtypeStruct((m, HALF), F32)],
        scratch_shapes=[pltpu.VMEM((SUBLANES, RWKV_COLS), F32)],
        compiler_params=_params("arbitrary"),
        name="rwkv_in",
    )(x2, w_r, mix, wl, w0a0, g2)


def _proj_body(x_ref, w_ref, o_ref):
    o_ref[...] = _dot(x_ref[...].astype(BF16), w_ref[...])


def _proj(x2, w, *, tm=512):
    m, k = x2.shape
    n = w.shape[1]
    return pl.pallas_call(
        _proj_body,
        grid=(m // tm,),
        in_specs=[pl.BlockSpec((tm, k), lambda i: (i, 0)), _const_spec(w.shape)],
        out_specs=pl.BlockSpec((tm, n), lambda i: (i, 0)),
        out_shape=jax.ShapeDtypeStruct((m, n), F32),
        compiler_params=_params("arbitrary"),
        name="proj",
    )(x2, w)


def _rwkv_scan_body(x_ref, kk_ref, ka_ref, rk_ref, lng_ref, lnb_ref, o_ref,
                    h_ref, av_ref, bv_ref, k2_ref, *, tb):
    @pl.when(pl.program_id(0) == 0)
    def _():
        h_ref[...] = jnp.zeros_like(h_ref)

    r = x_ref[:, 0]
    k = x_ref[:, 1]
    v = x_ref[:, 2]
    a = x_ref[:, 4]
    kk = k * kk_ref[...]
    nrm = jnp.sqrt(jnp.sum(kk * kk, axis=1, keepdims=True))
    kk = kk / jnp.maximum(nrm, 1e-12)
    k2 = k * (1.0 + (a - 1.0) * ka_ref[...])
    av_ref[...] = -kk
    bv_ref[...] = kk * a
    k2_ref[...] = k2
    bonus = jnp.sum(r * k2 * rk_ref[...], axis=1, keepdims=True) * v

    def step(t, carry):
        u = jnp.zeros((HEAD_DIM, LANES), F32)
        for j in range(HEAD_DIM):
            u = u + h_ref[j] * av_ref[t, pl.ds(j, 1), :]
        vt = x_ref[t, 2]
        y = jnp.zeros((HEAD_DIM, LANES), F32)
        for j in range(HEAD_DIM):
            wj = x_ref[t, 3, pl.ds(j, 1), :]
            rj = x_ref[t, 0, pl.ds(j, 1), :]
            bj = bv_ref[t, pl.ds(j, 1), :]
            kj = k2_ref[t, pl.ds(j, 1), :]
            hn = h_ref[j] * wj + u * bj + vt * kj
            h_ref[j] = hn
            y = y + hn * rj
        o_ref[t] = y
        return carry

    lax.fori_loop(0, tb, step, 0)

    y = o_ref[...]
    mu = jnp.mean(y, axis=1, keepdims=True)
    d = y - mu
    var = jnp.mean(d * d, axis=1, keepdims=True)
    o_ref[...] = d * lax.rsqrt(var + RWKV_GN_EPS) * lng_ref[...] + lnb_ref[...] + bonus


def _rwkv_scan(xs, kk, ka, rk, lng, lnb, *, tb=16):
    t = xs.shape[0]
    vec = pl.BlockSpec((HEAD_DIM, LANES), lambda i: (0, 0))
    blk = (tb, HEAD_DIM, LANES)
    return pl.pallas_call(
        functools.partial(_rwkv_scan_body, tb=tb),
        grid=(t // tb,),
        in_specs=[pl.BlockSpec((tb, 5, HEAD_DIM, LANES), lambda i: (i, 0, 0, 0)), vec, vec, vec, vec, vec],
        out_specs=pl.BlockSpec(blk, lambda i: (i, 0, 0)),
        out_shape=jax.ShapeDtypeStruct((t, HEAD_DIM, LANES), F32),
        scratch_shapes=[pltpu.VMEM((HEAD_DIM, HEAD_DIM, LANES), F32),
                        pltpu.VMEM(blk, F32), pltpu.VMEM(blk, F32), pltpu.VMEM(blk, F32)],
        compiler_params=_params("arbitrary"),
        name="rwkv_scan",
    )(xs, kk, ka, rk, lng, lnb)


def _ssd_body(xbc_ref, z_ref, dt_ref, dtt_ref, cw_ref, cb_ref, dtb_row_ref, dtb_col_ref,
              a_row_ref, a_col_ref, e_ref, dsk_ref, ng_ref, o_ref, ext_ref, st_ref):
    L = SSM_CHUNK

    @pl.when(pl.program_id(1) == 0)
    def _():
        ext_ref[0:SUBLANES, :] = jnp.zeros((SUBLANES, SSM_XBC), F32)
        st_ref[...] = jnp.zeros_like(st_ref)

    ext_ref[SUBLANES:SUBLANES + L, :] = xbc_ref[...]
    conv = cb_ref[...]
    for i in range(SSM_CONV):
        off = SUBLANES - (SSM_CONV - 1) + i
        conv = conv + cw_ref[i:i + 1, :] * ext_ref[off:off + L, :]
    ext_ref[0:SUBLANES, :] = ext_ref[L:L + SUBLANES, :]
    xc = _silu(conv)
    xs = xc[:, :HALF]
    bm = xc[:, HALF:HALF + SSM_GROUPS * SSM_STATE]
    cm = xc[:, HALF + SSM_GROUPS * SSM_STATE:]

    row = lax.broadcasted_iota(jnp.int32, (L, L), 0)
    col = lax.broadcasted_iota(jnp.int32, (L, L), 1)
    lower = row >= col
    dtc = _softplus(dt_ref[...] + dtb_row_ref[...])
    cum_col = _dot(lower.astype(F32), dtc * a_row_ref[...], _HI)
    dtr = _softplus(dtt_ref[0] + dtb_col_ref[...])
    cum_row = _dot(dtr * a_col_ref[...], (row <= col).astype(F32), _HI)
    dtx = _dot(dtc, e_ref[...], _HI)
    ccx = _dot(cum_col, e_ref[...], _HI)
    x = xs * dtx
    ecc = jnp.exp(ccx)
    clast = ccx[L - 1:L, :]
    xd = x * jnp.exp(clast - ccx)
    elast = jnp.exp(clast)
    lane = lax.broadcasted_iota(jnp.int32, (L, LANES), 1)

    ys = []
    for g in range(SSM_GROUPS):
        bg = bm[:, g * SSM_STATE:(g + 1) * SSM_STATE]
        cg = cm[:, g * SSM_STATE:(g + 1) * SSM_STATE].astype(BF16)
        cbm = _dot_nt(cg, bg.astype(BF16))
        bgt = bg.T.astype(BF16)
        for q in range(2):
            p = g * 2 + q
            sl = slice(p * LANES, (p + 1) * LANES)
            st = st_ref[p]
            yp = _dot(cg, st.astype(BF16)) * ecc[:, sl]
            xp = x[:, sl]
            for hh in range(2):
                h = 2 * p + hh
                seg = cum_col[:, h:h + 1] - cum_row[h:h + 1, :]
                dec = jnp.exp(jnp.where(lower, seg, NEG_BIG))
                xh = jnp.where((lane // HEAD_DIM) == hh, xp, 0.0).astype(BF16)
                yp = yp + _dot((cbm * dec).astype(BF16), xh)
            st_ref[p] = st * elast[:, sl] + _dot(bgt, xd[:, sl].astype(BF16))
            ys.append(yp)
    y = jnp.concatenate(ys, axis=1) + xs * dsk_ref[...]
    u = y * _silu(z_ref[...])
    gw = HALF // SSM_GROUPS
    outs = []
    for g in range(SSM_GROUPS):
        ug = u[:, g * gw:(g + 1) * gw]
        outs.append(ug * lax.rsqrt(jnp.mean(ug * ug, axis=-1, keepdims=True) + 1e-5))
    o_ref[...] = jnp.concatenate(outs, axis=1) * ng_ref[...]


def _ssd(ps, dtt, cw, cb, dtb_row, dtb_col, a_row, a_col, e, dsk, ng, *, batch, seq):
    L = SSM_CHUNK
    nc = seq // L
    consts = [cw, cb, dtb_row, dtb_col, a_row, a_col, e, dsk, ng]
    return pl.pallas_call(
        _ssd_body,
        grid=(batch, nc),
        in_specs=[
            pl.BlockSpec((L, SSM_XBC), lambda b, c: (b * nc + c, 0)),
            pl.BlockSpec((L, HALF), lambda b, c: (b * nc + c, SSM_XBC // HALF)),
            pl.BlockSpec((L, LANES), lambda b, c: (b * nc + c, (SSM_XBC + HALF) // LANES)),
            pl.BlockSpec((1, N_HEADS, L), lambda b, c: (b, 0, c)),
        ] + [_const_spec(c.shape) for c in consts],
        out_specs=pl.BlockSpec((L, HALF), lambda b, c: (b * nc + c, 0)),
        out_shape=jax.ShapeDtypeStruct((batch * seq, HALF), F32),
        scratch_shapes=[pltpu.VMEM((L + 2 * SUBLANES, SSM_XBC), F32),
                        pltpu.VMEM((N_HEADS // 2, SSM_STATE, LANES), F32)],
        compiler_params=_params("arbitrary", "arbitrary"),
        name="ssd",
    )(ps, ps, ps, dtt, *consts)


def _mix_out_body(*refs, gated):
    if gated:
        ya_ref, g_ref, yb_ref, x_ref, w_ref, lng_ref, lnb_ref, o_ref = refs
        ya = ya_ref[...] * g_ref[...]
    else:
        ya_ref, yb_ref, x_ref, w_ref, lng_ref, lnb_ref, o_ref = refs
        ya = ya_ref[...]
    mixed = _dot(ya.astype(BF16), w_ref[0:HALF, :]) + _dot(yb_ref[...].astype(BF16), w_ref[HALF:, :])
    o_ref[...] = _layer_norm(ALPHA * x_ref[...] + mixed, lng_ref[...], lnb_ref[...])


def _mix_out(ya, g, yb, x2, w, lng, lnb, *, tm=512):
    m = x2.shape[0]
    half = pl.BlockSpec((tm, HALF), lambda i: (i, 0))
    full = pl.BlockSpec((tm, D_MODEL), lambda i: (i, 0))
    gated = g is not None
    acts = [ya, g, yb] if gated else [ya, yb]
    return pl.pallas_call(
        functools.partial(_mix_out_body, gated=gated),
        grid=(m // tm,),
        in_specs=[half] * len(acts) + [full, _const_spec(w.shape), _const_spec(lng.shape), _const_spec(lnb.shape)],
        out_specs=full,
        out_shape=jax.ShapeDtypeStruct((m, D_MODEL), F32),
        compiler_params=_params("arbitrary"),
        name="mix_out",
    )(*acts, x2, w, lng, lnb)


def _ffn_body(x_ref, wg_ref, wu_ref, cw_ref, cb_ref, wd_ref, lng_ref, lnb_ref, o_ref,
              xb_ref, acc_ref, carry_ref, *, tiles_per_seq):
    i = pl.program_id(0)
    f = pl.program_id(1)
    nf = pl.num_programs(1)

    @pl.when(f == 0)
    def _():
        xb_ref[...] = x_ref[...].astype(BF16)
        acc_ref[...] = jnp.zeros_like(acc_ref)

    xb = xb_ref[...]
    gate = _dot(xb, wg_ref[...])
    up = _dot(xb, wu_ref[...])
    tm = gate.shape[0]
    first = (i % tiles_per_seq) == 0
    tail = jnp.where(first, 0.0, carry_ref[f])
    carry_ref[f] = gate[tm - SUBLANES:tm, :]
    row = lax.broadcasted_iota(jnp.int32, gate.shape, 0)
    conv = cb_ref[...] + cw_ref[FFN_CONV - 1:FFN_CONV, :] * gate
    for s in range(1, FFN_CONV):
        shifted = pltpu.roll(gate, s, 0)
        for r in range(s):
            shifted = jnp.where(row == r, tail[SUBLANES - s + r:SUBLANES - s + r + 1, :], shifted)
        conv = conv + cw_ref[FFN_CONV - 1 - s:FFN_CONV - s, :] * shifted
    act = (_silu(conv) * up).astype(BF16)
    acc_ref[...] += _dot(act, wd_ref[...])

    @pl.when(f == nf - 1)
    def _():
        o_ref[...] = _layer_norm(ALPHA * x_ref[...] + acc_ref[...], lng_ref[...], lnb_ref[...])


def _ffn(x2, wg, wu, cw, cb, wd, lng, lnb, *, seq, tm=1024, tf=256):
    m = x2.shape[0]
    tm = min(tm, seq)
    nf = D_FF // tf
    return pl.pallas_call(
        functools.partial(_ffn_body, tiles_per_seq=seq // tm),
        grid=(m // tm, nf),
        in_specs=[
            pl.BlockSpec((tm, D_MODEL), lambda i, f: (i, 0)),
            pl.BlockSpec((D_MODEL, tf), lambda i, f: (0, f)),
            pl.BlockSpec((D_MODEL, tf), lambda i, f: (0, f)),
            pl.BlockSpec((FFN_CONV, tf), lambda i, f: (0, f)),
            pl.BlockSpec((1, tf), lambda i, f: (0, f)),
            pl.BlockSpec((tf, D_MODEL), lambda i, f: (f, 0)),
            _const_spec(lng.shape), _const_spec(lnb.shape),
        ],
        out_specs=pl.BlockSpec((tm, D_MODEL), lambda i, f: (i, 0)),
        out_shape=jax.ShapeDtypeStruct((m, D_MODEL), F32),
        scratch_shapes=[pltpu.VMEM((tm, D_MODEL), BF16), pltpu.VMEM((tm, D_MODEL), F32),
                        pltpu.VMEM((nf, SUBLANES, tf), F32)],
        compiler_params=_params("arbitrary", "arbitrary"),
        name="conv_ffn",
    )(x2, wg, wu, cw, cb, wd, lng, lnb)


PAIR = 2 * LANES
N_PAIRS = N_HEADS // 2
L1_SB = 3 * HALF
L1_COLS_PAD = L1_SB + MLA_Q_LORA + MLA_KV_LORA + 2 * LANES


def _l1_in_body(x_ref, pos_ref, w_ref, invf_ref, qg_ref, wuq_ref, kvg_ref, wukv_ref,
                qsb_ref, ksb_ref, vsb_ref, q_ref, k_ref, v_ref):
    p = _dot(x_ref[...].astype(BF16), w_ref[...])
    qsb_ref[...] = p[:, 0:HALF].astype(BF16)
    ksb_ref[...] = p[:, HALF:2 * HALF].astype(BF16)
    vsb_ref[...] = p[:, 2 * HALF:3 * HALF].astype(BF16)
    c_q = p[:, L1_SB:L1_SB + MLA_Q_LORA]
    c_kv = p[:, L1_SB + MLA_Q_LORA:L1_SB + MLA_Q_LORA + MLA_KV_LORA]
    off = L1_SB + MLA_Q_LORA + MLA_KV_LORA
    kpe = p[:, off:off + LANES]
    kpe_rot = p[:, off + LANES:off + 2 * LANES]
    ang = pos_ref[...].astype(F32) * invf_ref[...]
    cos = jnp.cos(ang)
    sin = jnp.sin(ang)
    kpe = (kpe * cos + kpe_rot * sin).astype(BF16)
    cqn = c_q * lax.rsqrt(jnp.mean(c_q * c_q, axis=-1, keepdims=True) + 1e-6) * qg_ref[...]
    q = _dot(cqn.astype(BF16), wuq_ref[...])
    ckn = c_kv * lax.rsqrt(jnp.mean(c_kv * c_kv, axis=-1, keepdims=True) + 1e-6) * kvg_ref[...]
    kv = _dot(ckn.astype(BF16), wukv_ref[...])
    for pr in range(N_PAIRS):
        q_ref[:, pr * PAIR:pr * PAIR + LANES] = q[:, pr * PAIR:pr * PAIR + LANES].astype(BF16)
        q_pe = q[:, pr * PAIR + LANES:(pr + 1) * PAIR]
        q_rot = q[:, N_PAIRS * PAIR + pr * LANES:N_PAIRS * PAIR + (pr + 1) * LANES]
        q_ref[:, pr * PAIR + LANES:(pr + 1) * PAIR] = (q_pe * cos + q_rot * sin).astype(BF16)
        k_ref[:, pr * PAIR:pr * PAIR + LANES] = kv[:, pr * LANES:(pr + 1) * LANES].astype(BF16)
        k_ref[:, pr * PAIR + LANES:(pr + 1) * PAIR] = kpe
    v_ref[...] = kv[:, HALF:].astype(BF16)


def _l1_in(x2, pos2, w, invf, qg, wuq, kvg, wukv, *, tm=512):
    m = x2.shape[0]
    consts = [w, invf, qg, wuq, kvg, wukv]
    half = pl.BlockSpec((tm, HALF), lambda i: (i, 0))
    cat = pl.BlockSpec((tm, N_PAIRS * PAIR), lambda i: (i, 0))
    sd = jax.ShapeDtypeStruct
    return pl.pallas_call(
        _l1_in_body,
        grid=(m // tm,),
        in_specs=[pl.BlockSpec((tm, D_MODEL), lambda i: (i, 0)), pl.BlockSpec((tm, 1), lambda i: (i, 0))]
        + [_const_spec(c.shape) for c in consts],
        out_specs=[half, half, half, cat, cat, half],
        out_shape=[sd((m, HALF), BF16)] * 3 + [sd((m, N_PAIRS * PAIR), BF16)] * 2 + [sd((m, HALF), BF16)],
        compiler_params=_params("arbitrary"),
        name="l1_in",
    )(x2, pos2, *consts)


ATT_BLOCK = 128


def _sb_body(q_ref, k_ref, v_ref, o_ref):
    qi = pl.program_id(2)
    tq = ATT_BLOCK
    scale = HEAD_DIM ** -0.5
    q = q_ref[...]
    lane = lax.broadcasted_iota(jnp.int32, (tq, LANES), 1)
    qh = [jnp.where((lane // HEAD_DIM) == hh, q, jnp.zeros_like(q)) for hh in range(2)]
    row = lax.broadcasted_iota(jnp.int32, (tq, tq), 0)
    col = lax.broadcasted_iota(jnp.int32, (tq, tq), 1)
    later = (row > col).astype(BF16)

    def block(t, carry):
        kj = qi - t
        start = pl.multiple_of(kj * tq, tq)
        k = k_ref[pl.ds(start, tq), :]
        v = v_ref[pl.ds(start, tq), :]
        strict = (col + kj * tq) < (row + qi * tq)
        new = []
        for hh in range(2):
            acc, run = carry[hh]
            z = _dot_nt(qh[hh], k) * scale
            log_beta = jnp.minimum(z, 0.0) - jnp.log1p(jnp.exp(-jnp.abs(z)))
            log_keep = jnp.where(strict, log_beta - z, 0.0)
            hi = log_keep.astype(BF16)
            lo = (log_keep - hi.astype(F32)).astype(BF16)
            suffix = _dot(hi, later) + _dot(lo, later)
            att = jnp.where(strict, jnp.exp(log_beta + suffix + run), 0.0)
            vh = jnp.where((lane // HEAD_DIM) == hh, v, jnp.zeros_like(v))
            acc = acc + _dot(att.astype(BF16), vh)
            run = run + jnp.sum(log_keep, axis=1, keepdims=True)
            new.append((acc, run))
        return tuple(new)

    init = tuple((jnp.zeros((tq, LANES), F32), jnp.zeros((tq, 1), F32)) for _ in range(2))
    out = lax.fori_loop(0, qi + 1, block, init)
    o_ref[...] = out[0][0] + out[1][0]


def _sb_attention(q, k, v, *, batch, seq):
    tq = ATT_BLOCK
    nq = seq // tq
    return pl.pallas_call(
        _sb_body,
        grid=(batch, N_PAIRS, nq),
        in_specs=[
            pl.BlockSpec((tq, LANES), lambda b, p, i: (b * nq + i, p)),
            pl.BlockSpec((seq, LANES), lambda b, p, i: (b, p)),
            pl.BlockSpec((seq, LANES), lambda b, p, i: (b, p)),
        ],
        out_specs=pl.BlockSpec((tq, LANES), lambda b, p, i: (b * nq + i, p)),
        out_shape=jax.ShapeDtypeStruct((batch * seq, HALF), F32),
        compiler_params=_params("arbitrary", "arbitrary", "arbitrary"),
        name="sb_attention",
    )(q, k, v)


def _mla_body(q_ref, k_ref, v_ref, o_ref):
    qi = pl.program_id(2)
    tq = ATT_BLOCK
    scale = (MLA_NOPE + MLA_ROPE) ** -0.5
    q = q_ref[...]
    lane2 = lax.broadcasted_iota(jnp.int32, (tq, PAIR), 1)
    lane = lax.broadcasted_iota(jnp.int32, (tq, LANES), 1)
    qh = []
    for hh in range(2):
        mine = ((lane2 >= hh * MLA_NOPE) & (lane2 < (hh + 1) * MLA_NOPE)) | (
            (lane2 >= LANES + hh * MLA_ROPE) & (lane2 < LANES + (hh + 1) * MLA_ROPE))
        qh.append(jnp.where(mine, q, jnp.zeros_like(q)))
    row = lax.broadcasted_iota(jnp.int32, (tq, tq), 0)
    col = lax.broadcasted_iota(jnp.int32, (tq, tq), 1)

    def block(kj, carry):
        start = pl.multiple_of(kj * tq, tq)
        k = k_ref[pl.ds(start, tq), :]
        v = v_ref[pl.ds(start, tq), :]
        causal = (col + kj * tq) <= (row + qi * tq)
        new = []
        for hh in range(2):
            acc, m, l = carry[hh]
            s = jnp.where(causal, _dot_nt(qh[hh], k) * scale, NEG_BIG)
            m_new = jnp.maximum(m, jnp.max(s, axis=1, keepdims=True))
            corr = jnp.exp(m - m_new)
            pexp = jnp.exp(s - m_new)
            vh = jnp.where((lane // MLA_V) == hh, v, jnp.zeros_like(v))
            acc = acc * corr + _dot(pexp.astype(BF16), vh)
            l = l * corr + jnp.sum(pexp, axis=1, keepdims=True)
            new.append((acc, m_new, l))
        return tuple(new)

    init = tuple((jnp.zeros((tq, LANES), F32), jnp.full((tq, 1), NEG_BIG, F32), jnp.zeros((tq, 1), F32))
                 for _ in range(2))
    out = lax.fori_loop(0, qi + 1, block, init)
    o_ref[...] = out[0][0] / out[0][2] + out[1][0] / out[1][2]


def _mla_attention(q, k, v, *, batch, seq):
    tq = ATT_BLOCK
    nq = seq // tq
    return pl.pallas_call(
        _mla_body,
        grid=(batch, N_PAIRS, nq),
        in_specs=[
            pl.BlockSpec((tq, PAIR), lambda b, p, i: (b * nq + i, p)),
            pl.BlockSpec((seq, PAIR), lambda b, p, i: (b, p)),
            pl.BlockSpec((seq, LANES), lambda b, p, i: (b, p)),
        ],
        out_specs=pl.BlockSpec((tq, LANES), lambda b, p, i: (b * nq + i, p)),
        out_shape=jax.ShapeDtypeStruct((batch * seq, HALF), F32),
        compiler_params=_params("arbitrary", "arbitrary", "arbitrary"),
        name="mla_attention",
    )(q, k, v)


def _row(v):
    return v.reshape(1, -1).astype(F32)


def _chain_layout(v):
    per_head = v.reshape(N_HEADS, HEAD_DIM).T
    return jnp.tile(per_head, (1, LANES // N_HEADS)).astype(F32)


def _rope_rotate_cols(w):
    half = MLA_ROPE // 2
    return jnp.concatenate([-w[..., half:], w[..., :half]], axis=-1)


def _layer0_mixer(x2, batch, seq, l0_w_in, rwkv_mix, rwkv_w0, rwkv_w2, rwkv_a0, rwkv_a2, rwkv_g2,
                  rwkv_k_k, rwkv_k_a, rwkv_r_k, rwkv_ln_g, rwkv_ln_b, ssm_conv_w, ssm_conv_b,
                  ssm_dt_bias, ssm_a_log, ssm_d, ssm_norm_g):
    w_r = l0_w_in[:, :RWKV_COLS].astype(BF16)
    zeros = jnp.zeros((RWKV_DECAY_LORA, HALF), F32)
    wl = jnp.concatenate([jnp.concatenate([rwkv_w2, zeros], axis=1),
                          jnp.concatenate([zeros, rwkv_a2], axis=1)], axis=0).astype(BF16)
    w0a0 = _row(jnp.concatenate([rwkv_w0, rwkv_a0]))
    rkvwa, gate = _rwkv_in(x2, w_r, _row(rwkv_mix), wl, w0a0, rwkv_g2.astype(BF16), seq=seq)
    xs = rkvwa.reshape(batch, seq, 5, N_HEADS, HEAD_DIM).transpose(1, 2, 4, 0, 3)
    xs = xs.reshape(seq, 5, HEAD_DIM, batch * N_HEADS)
    y = _rwkv_scan(xs, _chain_layout(rwkv_k_k), _chain_layout(rwkv_k_a), _chain_layout(rwkv_r_k.reshape(-1)),
                   _chain_layout(rwkv_ln_g), _chain_layout(rwkv_ln_b))
    y_a = y.reshape(seq, HEAD_DIM, batch, N_HEADS).transpose(2, 0, 3, 1).reshape(batch * seq, HALF)

    w_s = l0_w_in[:, RWKV_COLS:]
    pad = jnp.zeros((D_MODEL, LANES - N_HEADS), F32)
    w_s = jnp.concatenate([w_s[:, HALF:HALF + SSM_XBC], w_s[:, :HALF], w_s[:, HALF + SSM_XBC:], pad], axis=1)
    ps = _proj(x2, w_s.astype(BF16))
    dt_raw = ps[:, SSM_XBC + HALF:SSM_XBC + HALF + N_HEADS]
    dtt = dt_raw.reshape(batch, seq, N_HEADS).transpose(0, 2, 1)
    a_neg = -jnp.exp(ssm_a_log.astype(F32))
    lane_pad = jnp.zeros((LANES - N_HEADS,), F32)
    expand = jnp.concatenate([jnp.repeat(jnp.eye(N_HEADS, dtype=F32), HEAD_DIM, axis=1),
                              jnp.zeros((LANES - N_HEADS, HALF), F32)], axis=0)
    y_b = _ssd(ps, dtt, ssm_conv_w.astype(F32), _row(ssm_conv_b),
               _row(jnp.concatenate([ssm_dt_bias, lane_pad])), ssm_dt_bias.reshape(N_HEADS, 1).astype(F32),
               _row(jnp.concatenate([a_neg, lane_pad])), a_neg.reshape(N_HEADS, 1),
               expand, _row(jnp.repeat(ssm_d, HEAD_DIM)), _row(ssm_norm_g), batch=batch, seq=seq)
    return y_a, gate, y_b


def _layer1_mixer(h2, positions, batch, seq, l1_w_in, mla_q_norm_g, mla_w_uq, mla_kv_norm_g, mla_w_ukv):
    off = L1_SB + MLA_Q_LORA + MLA_KV_LORA
    w_kpe = l1_w_in[:, off:off + MLA_ROPE]
    lane_zeros = jnp.zeros((D_MODEL, LANES - 2 * MLA_ROPE), F32)
    w_kpe_blk = jnp.concatenate([w_kpe, w_kpe, lane_zeros], axis=1)
    w_kpe_rot = _rope_rotate_cols(w_kpe)
    w_kpe_rot_blk = jnp.concatenate([w_kpe_rot, w_kpe_rot, lane_zeros], axis=1)
    w_in = jnp.concatenate([l1_w_in[:, :off], w_kpe_blk, w_kpe_rot_blk], axis=1).astype(BF16)

    inv_freq = 1.0 / (ROPE_THETA ** (jnp.arange(0, MLA_ROPE, 2, dtype=F32) / MLA_ROPE))
    invf = jnp.concatenate([jnp.tile(inv_freq, 4), jnp.zeros((LANES - 2 * MLA_ROPE,), F32)]).reshape(1, LANES)

    wq = mla_w_uq.reshape(MLA_Q_LORA, N_HEADS, MLA_NOPE + MLA_ROPE)
    wq_nope = wq[:, :, :MLA_NOPE].reshape(MLA_Q_LORA, N_PAIRS, 2 * MLA_NOPE)
    wq_pe = wq[:, :, MLA_NOPE:]
    lz = jnp.zeros((MLA_Q_LORA, N_PAIRS, LANES - 2 * MLA_ROPE), F32)
    wq_pe_blk = jnp.concatenate([wq_pe.reshape(MLA_Q_LORA, N_PAIRS, 2 * MLA_ROPE), lz], axis=2)
    wq_rot_blk = jnp.concatenate([_rope_rotate_cols(wq_pe).reshape(MLA_Q_LORA, N_PAIRS, 2 * MLA_ROPE), lz], axis=2)
    wuq = jnp.concatenate([jnp.concatenate([wq_nope, wq_pe_blk], axis=2).reshape(MLA_Q_LORA, N_PAIRS * PAIR),
                           wq_rot_blk.reshape(MLA_Q_LORA, N_PAIRS * LANES)], axis=1).astype(BF16)

    wkv = mla_w_ukv.reshape(MLA_KV_LORA, N_HEADS, MLA_NOPE + MLA_V)
    wukv = jnp.concatenate([wkv[:, :, :MLA_NOPE].reshape(MLA_KV_LORA, HALF),
                            wkv[:, :, MLA_NOPE:].reshape(MLA_KV_LORA, HALF)], axis=1).astype(BF16)

    qsb, ksb, vsb, q, k, v = _l1_in(h2, positions.reshape(batch * seq, 1), w_in, invf,
                                    _row(mla_q_norm_g), wuq, _row(mla_kv_norm_g), wukv)
    y_c = _sb_attention(qsb, ksb, vsb, batch=batch, seq=seq)
    y_d = _mla_attention(q, k, v, batch=batch, seq=seq)
    return y_c, y_d


def _ffn_layer(h2, seq, w_up, conv_w, conv_b, w_down, ln_g, ln_b):
    return _ffn(h2, w_up[:, :D_FF].astype(BF16), w_up[:, D_FF:].astype(BF16), conv_w.astype(F32),
                _row(conv_b), w_down.astype(BF16), _row(ln_g), _row(ln_b), seq=seq)


def kernel(x, positions, l0_w_in, rwkv_mix, rwkv_w0, rwkv_w2, rwkv_a0, rwkv_a2, rwkv_g2, rwkv_k_k, rwkv_k_a, rwkv_r_k, rwkv_ln_g, rwkv_ln_b, ssm_conv_w, ssm_conv_b, ssm_dt_bias, ssm_a_log, ssm_d, ssm_norm_g, l0_w_out, l0_ln1_g, l0_ln1_b, ffn0_w_up, ffn0_conv_w, ffn0_conv_b, ffn0_w_down, l0_ln2_g, l0_ln2_b, l1_w_in, mla_q_norm_g, mla_w_uq, mla_kv_norm_g, mla_w_ukv, l1_w_out, l1_ln1_g, l1_ln1_b, ffn1_w_up, ffn1_conv_w, ffn1_conv_b, ffn1_w_down, l1_ln2_g, l1_ln2_b):
    batch, seq, _ = x.shape
    assert batch * N_HEADS == LANES, "the RWKV scan maps batch*heads onto the lane axis"
    x2 = x.reshape(batch * seq, D_MODEL).astype(F32)

    y_a, gate, y_b = _layer0_mixer(x2, batch, seq, l0_w_in, rwkv_mix, rwkv_w0, rwkv_w2, rwkv_a0, rwkv_a2,
                                   rwkv_g2, rwkv_k_k, rwkv_k_a, rwkv_r_k, rwkv_ln_g, rwkv_ln_b, ssm_conv_w,
                                   ssm_conv_b, ssm_dt_bias, ssm_a_log, ssm_d, ssm_norm_g)
    h = _mix_out(y_a, gate, y_b, x2, l0_w_out.astype(BF16), _row(l0_ln1_g), _row(l0_ln1_b))
    h = _ffn_layer(h, seq, ffn0_w_up, ffn0_conv_w, ffn0_conv_b, ffn0_w_down, l0_ln2_g, l0_ln2_b)

    y_c, y_d = _layer1_mixer(h, positions, batch, seq, l1_w_in, mla_q_norm_g, mla_w_uq, mla_kv_norm_g, mla_w_ukv)
    h = _mix_out(y_c, None, y_d, h, l1_w_out.astype(BF16), _row(l1_ln1_g), _row(l1_ln1_b))
    h = _ffn_layer(h, seq, ffn1_w_up, ffn1_conv_w, ffn1_conv_b, ffn1_w_down, l1_ln2_g, l1_ln2_b)
    return h.reshape(batch, seq, D_MODEL).astype(x.dtype)
```

```python
import functools
import math

import jax
import jax.numpy as jnp
from jax import lax
from jax.experimental import pallas as pl
from jax.experimental.pallas import tpu as pltpu

F32 = jnp.float32
BF16 = jnp.bfloat16

D_MODEL = 1024
HEAD_DIM = 64
N_HEADS = 8
HALF = N_HEADS * HEAD_DIM
RWKV_DECAY_LORA = 64
RWKV_A_LORA = 64
RWKV_GATE_LORA = 128
RWKV_GN_EPS = 64e-5
RWKV_COLS = 3 * HALF + RWKV_DECAY_LORA + RWKV_A_LORA + RWKV_GATE_LORA
SSM_GROUPS = 2
SSM_STATE = 128
SSM_CONV = 4
SSM_CHUNK = 128
SSM_XBC = HALF + 2 * SSM_GROUPS * SSM_STATE
SSM_COLS = HALF + SSM_XBC + N_HEADS
SSM_COLS_PAD = SSM_XBC + HALF + 128
MLA_NOPE = 64
MLA_ROPE = 32
MLA_V = 64
MLA_Q_LORA = 256
MLA_KV_LORA = 128
ROPE_THETA = 10000.0
D_FF = 2816
FFN_CONV = 3
DEPTH = 2
ALPHA = (2 * DEPTH) ** 0.25
LN_EPS = 1e-5

LANES = 128
SUBLANES = 8
VMEM_LIMIT = 56 * 1024 * 1024
NEG_BIG = -1e30

_HI = lax.Precision.HIGHEST


def _params(*sem):
    return pltpu.CompilerParams(dimension_semantics=sem, vmem_limit_bytes=VMEM_LIMIT)


def _dot(a, b, precision=None):
    return jnp.dot(a, b, preferred_element_type=F32, precision=precision)


def _dot_nt(a, b):
    return lax.dot_general(a, b, (((1,), (1,)), ((), ())), preferred_element_type=F32)


def _sigmoid(x):
    return 1.0 / (1.0 + jnp.exp(-x))


def _softplus(x):
    return jnp.maximum(x, 0.0) + jnp.log1p(jnp.exp(-jnp.abs(x)))


def _silu(x):
    return x * _sigmoid(x)


def _layer_norm(v, g, b):
    mu = jnp.mean(v, axis=-1, keepdims=True)
    d = v - mu
    var = jnp.mean(d * d, axis=-1, keepdims=True)
    return d * lax.rsqrt(var + LN_EPS) * g + b


def _const_spec(shape):
    nd = len(shape)
    return pl.BlockSpec(shape, lambda *_: (0,) * nd)


def _rwkv_in_body(x_ref, w_ref, mix_ref, wl_ref, w0a0_ref, g2_ref, o_ref, g_ref, carry_ref, *, tiles_per_seq):
    i = pl.program_id(0)
    p = _dot(x_ref[...].astype(BF16), w_ref[...])
    tm = p.shape[0]
    first = (i % tiles_per_seq) == 0
    prev_row = jnp.where(first, 0.0, carry_ref[SUBLANES - 1:SUBLANES, :])
    row = lax.broadcasted_iota(jnp.int32, p.shape, 0)
    prev = jnp.where(row == 0, prev_row, pltpu.roll(p, 1, 0))
    carry_ref[...] = p[tm - SUBLANES:tm, :]
    pm = p + (prev - p) * mix_ref[...]
    lo = pm[:, 3 * HALF:3 * HALF + LANES]
    lane = lax.broadcasted_iota(jnp.int32, lo.shape, 1)
    lo = jnp.where(lane < RWKV_DECAY_LORA, jnp.tanh(lo), lo)
    wa = _dot(lo.astype(BF16), wl_ref[...]) + w0a0_ref[...]
    log_w = -_softplus(-wa[:, :HALF]) - 0.5
    o_ref[:, 0:3 * HALF] = pm[:, 0:3 * HALF]
    o_ref[:, 3 * HALF:4 * HALF] = jnp.exp(-jnp.exp(log_w))
    o_ref[:, 4 * HALF:5 * HALF] = _sigmoid(wa[:, HALF:])
    g_lo = _sigmoid(pm[:, 3 * HALF + LANES:])
    g_ref[...] = _dot(g_lo.astype(BF16), g2_ref[...])


def _rwkv_in(x2, w_r, mix, wl, w0a0, g2, *, seq, tm=256):
    m = x2.shape[0]
    body = functools.partial(_rwkv_in_body, tiles_per_seq=seq // tm)
    return pl.pallas_call(
        body,
        grid=(m // tm,),
        in_specs=[
            pl.BlockSpec((tm, D_MODEL), lambda i: (i, 0)),
            _const_spec(w_r.shape), _const_spec(mix.shape), _const_spec(wl.shape),
            _const_spec(w0a0.shape), _const_spec(g2.shape),
        ],
        out_specs=[
            pl.BlockSpec((tm, 5 * HALF), lambda i: (i, 0)),
            pl.BlockSpec((tm, HALF), lambda i: (i, 0)),
        ],
        out_shape=[jax.ShapeDtypeStruct((m, 5 * HALF), F32), jax.ShapeDtypeStruct((m, HALF), F32)],
        scratch_shapes=[pltpu.VMEM((SUBLANES, RWKV_COLS), F32)],
        compiler_params=_params("arbitrary"),
        name="rwkv_in",
    )(x2, w_r, mix, wl, w0a0, g2)


def _proj_body(x_ref, w_ref, o_ref):
    o_ref[...] = _dot(x_ref[...].astype(BF16), w_ref[...])


def _proj(x2, w, *, tm=512):
    m, k = x2.shape
    n = w.shape[1]
    return pl.pallas_call(
        _proj_body,
        grid=(m // tm,),
        in_specs=[pl.BlockSpec((tm, k), lambda i: (i, 0)), _const_spec(w.shape)],
        out_specs=pl.BlockSpec((tm, n), lambda i: (i, 0)),
        out_shape=jax.ShapeDtypeStruct((m, n), F32),
        compiler_params=_params("arbitrary"),
        name="proj",
    )(x2, w)


def _rwkv_scan_body(x_ref, kk_ref, ka_ref, rk_ref, lng_ref, lnb_ref, o_ref,
                    h_ref, av_ref, bv_ref, k2_ref, *, tb):
    @pl.when(pl.program_id(0) == 0)
    def _():
        h_ref[...] = jnp.zeros_like(h_ref)

    r = x_ref[:, 0]
    k = x_ref[:, 1]
    v = x_ref[:, 2]
    a = x_ref[:, 4]
    kk = k * kk_ref[...]
    nrm = jnp.sqrt(jnp.sum(kk * kk, axis=1, keepdims=True))
    kk = kk / jnp.maximum(nrm, 1e-12)
    k2 = k * (1.0 + (a - 1.0) * ka_ref[...])
    av_ref[...] = -kk
    bv_ref[...] = kk * a
    k2_ref[...] = k2
    bonus = jnp.sum(r * k2 * rk_ref[...], axis=1, keepdims=True) * v

    def step(t, carry):
        u = jnp.zeros((HEAD_DIM, LANES), F32)
        for j in range(HEAD_DIM):
            u = u + h_ref[j] * av_ref[t, pl.ds(j, 1), :]
        vt = x_ref[t, 2]
        y = jnp.zeros((HEAD_DIM, LANES), F32)
        for j in range(HEAD_DIM):
            wj = x_ref[t, 3, pl.ds(j, 1), :]
            rj = x_ref[t, 0, pl.ds(j, 1), :]
            bj = bv_ref[t, pl.ds(j, 1), :]
            kj = k2_ref[t, pl.ds(j, 1), :]
            hn = h_ref[j] * wj + u * bj + vt * kj
            h_ref[j] = hn
            y = y + hn * rj
        o_ref[t] = y
        return carry

    lax.fori_loop(0, tb, step, 0)

    y = o_ref[...]
    mu = jnp.mean(y, axis=1, keepdims=True)
    d = y - mu
    var = jnp.mean(d * d, axis=1, keepdims=True)
    o_ref[...] = d * lax.rsqrt(var + RWKV_GN_EPS) * lng_ref[...] + lnb_ref[...] + bonus


def _rwkv_scan(xs, kk, ka, rk, lng, lnb, *, tb=16):
    t = xs.shape[0]
    vec = pl.BlockSpec((HEAD_DIM, LANES), lambda i: (0, 0))
    blk = (tb, HEAD_DIM, LANES)
    return pl.pallas_call(
        functools.partial(_rwkv_scan_body, tb=tb),
        grid=(t // tb,),
        in_specs=[pl.BlockSpec((tb, 5, HEAD_DIM, LANES), lambda i: (i, 0, 0, 0)), vec, vec, vec, vec, vec],
        out_specs=pl.BlockSpec(blk, lambda i: (i, 0, 0)),
        out_shape=jax.ShapeDtypeStruct((t, HEAD_DIM, LANES), F32),
        scratch_shapes=[pltpu.VMEM((HEAD_DIM, HEAD_DIM, LANES), F32),
                        pltpu.VMEM(blk, F32), pltpu.VMEM(blk, F32), pltpu.VMEM(blk, F32)],
        compiler_params=_params("arbitrary"),
        name="rwkv_scan",
    )(xs, kk, ka, rk, lng, lnb)


def _ssd_body(xbc_ref, z_ref, dt_ref, dtt_ref, cw_ref, cb_ref, dtb_row_ref, dtb_col_ref,
              a_row_ref, a_col_ref, e_ref, dsk_ref, ng_ref, o_ref, ext_ref, st_ref):
    L = SSM_CHUNK

    @pl.when(pl.program_id(1) == 0)
    def _():
        ext_ref[0:SUBLANES, :] = jnp.zeros((SUBLANES, SSM_XBC), F32)
        st_ref[...] = jnp.zeros_like(st_ref)

    ext_ref[SUBLANES:SUBLANES + L, :] = xbc_ref[...]
    conv = cb_ref[...]
    for i in range(SSM_CONV):
        off = SUBLANES - (SSM_CONV - 1) + i
        conv = conv + cw_ref[i:i + 1, :] * ext_ref[off:off + L, :]
    ext_ref[0:SUBLANES, :] = ext_ref[L:L + SUBLANES, :]
    xc = _silu(conv)
    xs = xc[:, :HALF]
    bm = xc[:, HALF:HALF + SSM_GROUPS * SSM_STATE]
    cm = xc[:, HALF + SSM_GROUPS * SSM_STATE:]

    row = lax.broadcasted_iota(jnp.int32, (L, L), 0)
    col = lax.broadcasted_iota(jnp.int32, (L, L), 1)
    lower = row >= col
    dtc = _softplus(dt_ref[...] + dtb_row_ref[...])
    cum_col = _dot(lower.astype(F32), dtc * a_row_ref[...], _HI)
    dtr = _softplus(dtt_ref[0] + dtb_col_ref[...])
    cum_row = _dot(dtr * a_col_ref[...], (row <= col).astype(F32), _HI)
    dtx = _dot(dtc, e_ref[...], _HI)
    ccx = _dot(cum_col, e_ref[...], _HI)
    x = xs * dtx
    ecc = jnp.exp(ccx)
    clast = ccx[L - 1:L, :]
    xd = x * jnp.exp(clast - ccx)
    elast = jnp.exp(clast)
    lane = lax.broadcasted_iota(jnp.int32, (L, LANES), 1)

    ys = []
    for g in range(SSM_GROUPS):
        bg = bm[:, g * SSM_STATE:(g + 1) * SSM_STATE]
        cg = cm[:, g * SSM_STATE:(g + 1) * SSM_STATE].astype(BF16)
        cbm = _dot_nt(cg, bg.astype(BF16))
        bgt = bg.T.astype(BF16)
        for q in range(2):
            p = g * 2 + q
            sl = slice(p * LANES, (p + 1) * LANES)
            st = st_ref[p]
            yp = _dot(cg, st.astype(BF16)) * ecc[:, sl]
            xp = x[:, sl]
            for hh in range(2):
                h = 2 * p + hh
                seg = cum_col[:, h:h + 1] - cum_row[h:h + 1, :]
                dec = jnp.exp(jnp.where(lower, seg, NEG_BIG))
                xh = jnp.where((lane // HEAD_DIM) == hh, xp, 0.0).astype(BF16)
                yp = yp + _dot((cbm * dec).astype(BF16), xh)
            st_ref[p] = st * elast[:, sl] + _dot(bgt, xd[:, sl].astype(BF16))
            ys.append(yp)
    y = jnp.concatenate(ys, axis=1) + xs * dsk_ref[...]
    u = y * _silu(z_ref[...])
    gw = HALF // SSM_GROUPS
    outs = []
    for g in range(SSM_GROUPS):
        ug = u[:, g * gw:(g + 1) * gw]
        outs.append(ug * lax.rsqrt(jnp.mean(ug * ug, axis=-1, keepdims=True) + 1e-5))
    o_ref[...] = jnp.concatenate(outs, axis=1) * ng_ref[...]


def _ssd(ps, dtt, cw, cb, dtb_row, dtb_col, a_row, a_col, e, dsk, ng, *, batch, seq):
    L = SSM_CHUNK
    nc = seq // L
    consts = [cw, cb, dtb_row, dtb_col, a_row, a_col, e, dsk, ng]
    return pl.pallas_call(
        _ssd_body,
        grid=(batch, nc),
        in_specs=[
            pl.BlockSpec((L, SSM_XBC), lambda b, c: (b * nc + c, 0)),
            pl.BlockSpec((L, HALF), lambda b, c: (b * nc + c, SSM_XBC // HALF)),
            pl.BlockSpec((L, LANES), lambda b, c: (b * nc + c, (SSM_XBC + HALF) // LANES)),
            pl.BlockSpec((1, N_HEADS, L), lambda b, c: (b, 0, c)),
        ] + [_const_spec(c.shape) for c in consts],
        out_specs=pl.BlockSpec((L, HALF), lambda b, c: (b * nc + c, 0)),
        out_shape=jax.ShapeDtypeStruct((batch * seq, HALF), F32),
        scratch_shapes=[pltpu.VMEM((L + 2 * SUBLANES, SSM_XBC), F32),
                        pltpu.VMEM((N_HEADS // 2, SSM_STATE, LANES), F32)],
        compiler_params=_params("arbitrary", "arbitrary"),
        name="ssd",
    )(ps, ps, ps, dtt, *consts)


def _mix_out_body(*refs, gated):
    if gated:
        ya_ref, g_ref, yb_ref, x_ref, w_ref, lng_ref, lnb_ref, o_ref = refs
        ya = ya_ref[...] * g_ref[...]
    else:
        ya_ref, yb_ref, x_ref, w_ref, lng_ref, lnb_ref, o_ref = refs
        ya = ya_ref[...]
    mixed = _dot(ya.astype(BF16), w_ref[0:HALF, :]) + _dot(yb_ref[...].astype(BF16), w_ref[HALF:, :])
    o_ref[...] = _layer_norm(ALPHA * x_ref[...] + mixed, lng_ref[...], lnb_ref[...])


def _mix_out(ya, g, yb, x2, w, lng, lnb, *, tm=512):
    m = x2.shape[0]
    half = pl.BlockSpec((tm, HALF), lambda i: (i, 0))
    full = pl.BlockSpec((tm, D_MODEL), lambda i: (i, 0))
    gated = g is not None
    acts = [ya, g, yb] if gated else [ya, yb]
    return pl.pallas_call(
        functools.partial(_mix_out_body, gated=gated),
        grid=(m // tm,),
        in_specs=[half] * len(acts) + [full, _const_spec(w.shape), _const_spec(lng.shape), _const_spec(lnb.shape)],
        out_specs=full,
        out_shape=jax.ShapeDtypeStruct((m, D_MODEL), F32),
        compiler_params=_params("arbitrary"),
        name="mix_out",
    )(*acts, x2, w, lng, lnb)


def _ffn_body(x_ref, wg_ref, wu_ref, cw_ref, cb_ref, wd_ref, lng_ref, lnb_ref, o_ref,
              xb_ref, acc_ref, carry_ref, *, tiles_per_seq):
    i = pl.program_id(0)
    f = pl.program_id(1)
    nf = pl.num_programs(1)

    @pl.when(f == 0)
    def _():
        xb_ref[...] = x_ref[...].astype(BF16)
        acc_ref[...] = jnp.zeros_like(acc_ref)

    xb = xb_ref[...]
    gate = _dot(xb, wg_ref[...])
    up = _dot(xb, wu_ref[...])
    tm = gate.shape[0]
    first = (i % tiles_per_seq) == 0
    tail = jnp.where(first, 0.0, carry_ref[f])
    carry_ref[f] = gate[tm - SUBLANES:tm, :]
    row = lax.broadcasted_iota(jnp.int32, gate.shape, 0)
    conv = cb_ref[...] + cw_ref[FFN_CONV - 1:FFN_CONV, :] * gate
    for s in range(1, FFN_CONV):
        shifted = pltpu.roll(gate, s, 0)
        for r in range(s):
            shifted = jnp.where(row == r, tail[SUBLANES - s + r:SUBLANES - s + r + 1, :], shifted)
        conv = conv + cw_ref[FFN_CONV - 1 - s:FFN_CONV - s, :] * shifted
    act = (_silu(conv) * up).astype(BF16)
    acc_ref[...] += _dot(act, wd_ref[...])

    @pl.when(f == nf - 1)
    def _():
        o_ref[...] = _layer_norm(ALPHA * x_ref[...] + acc_ref[...], lng_ref[...], lnb_ref[...])


def _ffn(x2, wg, wu, cw, cb, wd, lng, lnb, *, seq, tm=1024, tf=256):
    m = x2.shape[0]
    tm = min(tm, seq)
    nf = D_FF // tf
    return pl.pallas_call(
        functools.partial(_ffn_body, tiles_per_seq=seq // tm),
        grid=(m // tm, nf),
        in_specs=[
            pl.BlockSpec((tm, D_MODEL), lambda i, f: (i, 0)),
            pl.BlockSpec((D_MODEL, tf), lambda i, f: (0, f)),
            pl.BlockSpec((D_MODEL, tf), lambda i, f: (0, f)),
            pl.BlockSpec((FFN_CONV, tf), lambda i, f: (0, f)),
            pl.BlockSpec((1, tf), lambda i, f: (0, f)),
            pl.BlockSpec((tf, D_MODEL), lambda i, f: (f, 0)),
            _const_spec(lng.shape), _const_spec(lnb.shape),
        ],
        out_specs=pl.BlockSpec((tm, D_MODEL), lambda i, f: (i, 0)),
        out_shape=jax.ShapeDtypeStruct((m, D_MODEL), F32),
        scratch_shapes=[pltpu.VMEM((tm, D_MODEL), BF16), pltpu.VMEM((tm, D_MODEL), F32),
                        pltpu.VMEM((nf, SUBLANES, tf), F32)],
        compiler_params=_params("arbitrary", "arbitrary"),
        name="conv_ffn",
    )(x2, wg, wu, cw, cb, wd, lng, lnb)


PAIR = 2 * LANES
N_PAIRS = N_HEADS // 2
L1_SB = 3 * HALF
L1_COLS_PAD = L1_SB + MLA_Q_LORA + MLA_KV_LORA + 2 * LANES


def _l1_in_body(x_ref, pos_ref, w_ref, invf_ref, qg_ref, wuq_ref, kvg_ref, wukv_ref,
                qsb_ref, ksb_ref, vsb_ref, q_ref, k_ref, v_ref):
    p = _dot(x_ref[...].astype(BF16), w_ref[...])
    qsb_ref[...] = p[:, 0:HALF].astype(BF16)
    ksb_ref[...] = p[:, HALF:2 * HALF].astype(BF16)
    vsb_ref[...] = p[:, 2 * HALF:3 * HALF].astype(BF16)
    c_q = p[:, L1_SB:L1_SB + MLA_Q_LORA]
    c_kv = p[:, L1_SB + MLA_Q_LORA:L1_SB + MLA_Q_LORA + MLA_KV_LORA]
    off = L1_SB + MLA_Q_LORA + MLA_KV_LORA
    kpe = p[:, off:off + LANES]
    kpe_rot = p[:, off + LANES:off + 2 * LANES]
    ang = pos_ref[...].astype(F32) * invf_ref[...]
    cos = jnp.cos(ang)
    sin = jnp.sin(ang)
    kpe = (kpe * cos + kpe_rot * sin).astype(BF16)
    cqn = c_q * lax.rsqrt(jnp.mean(c_q * c_q, axis=-1, keepdims=True) + 1e-6) * qg_ref[...]
    q = _dot(cqn.astype(BF16), wuq_ref[...])
    ckn = c_kv * lax.rsqrt(jnp.mean(c_kv * c_kv, axis=-1, keepdims=True) + 1e-6) * kvg_ref[...]
    kv = _dot(ckn.astype(BF16), wukv_ref[...])
    for pr in range(N_PAIRS):
        q_ref[:, pr * PAIR:pr * PAIR + LANES] = q[:, pr * PAIR:pr * PAIR + LANES].astype(BF16)
        q_pe = q[:, pr * PAIR + LANES:(pr + 1) * PAIR]
        q_rot = q[:, N_PAIRS * PAIR + pr * LANES:N_PAIRS * PAIR + (pr + 1) * LANES]
        q_ref[:, pr * PAIR + LANES:(pr + 1) * PAIR] = (q_pe * cos + q_rot * sin).astype(BF16)
        k_ref[:, pr * PAIR:pr * PAIR + LANES] = kv[:, pr * LANES:(pr + 1) * LANES].astype(BF16)
        k_ref[:, pr * PAIR + LANES:(pr + 1) * PAIR] = kpe
    v_ref[...] = kv[:, HALF:].astype(BF16)


def _l1_in(x2, pos2, w, invf, qg, wuq, kvg, wukv, *, tm=512):
    m = x2.shape[0]
    consts = [w, invf, qg, wuq, kvg, wukv]
    half = pl.BlockSpec((tm, HALF), lambda i: (i, 0))
    cat = pl.BlockSpec((tm, N_PAIRS * PAIR), lambda i: (i, 0))
    sd = jax.ShapeDtypeStruct
    return pl.pallas_call(
        _l1_in_body,
        grid=(m // tm,),
        in_specs=[pl.BlockSpec((tm, D_MODEL), lambda i: (i, 0)), pl.BlockSpec((tm, 1), lambda i: (i, 0))]
        + [_const_spec(c.shape) for c in consts],
        out_specs=[half, half, half, cat, cat, half],
        out_shape=[sd((m, HALF), BF16)] * 3 + [sd((m, N_PAIRS * PAIR), BF16)] * 2 + [sd((m, HALF), BF16)],
        compiler_params=_params("arbitrary"),
        name="l1_in",
    )(x2, pos2, *consts)


ATT_BLOCK = 256


def _stack_heads(q_ref, qs_ref, masks, width):
    tq = ATT_BLOCK
    for p in range(N_PAIRS):
        qp = q_ref[:, p * width:(p + 1) * width]
        for hh in range(2):
            qs_ref[p, hh * tq:(hh + 1) * tq, :] = jnp.where(masks[hh], qp, jnp.zeros_like(qp))


def _head_values(v):
    lane = lax.broadcasted_iota(jnp.int32, v.shape, 1)
    return [jnp.where((lane // HEAD_DIM) == hh, v, jnp.zeros_like(v)) for hh in range(2)]


def _sb_body(q_ref, k_ref, v_ref, o_ref, qs_ref, acc_ref):
    qi = pl.program_id(1)
    tq = ATT_BLOCK
    scale = HEAD_DIM ** -0.5
    lane = lax.broadcasted_iota(jnp.int32, (tq, LANES), 1)
    _stack_heads(q_ref, qs_ref, [(lane // HEAD_DIM) == hh for hh in range(2)], LANES)
    acc_ref[...] = jnp.zeros_like(acc_ref)
    row = lax.broadcasted_iota(jnp.int32, (2 * tq, tq), 0) & (tq - 1)
    col = lax.broadcasted_iota(jnp.int32, (2 * tq, tq), 1)
    strict = col < row
    later = (lax.broadcasted_iota(jnp.int32, (tq, tq), 0)
             > lax.broadcasted_iota(jnp.int32, (tq, tq), 1)).astype(BF16)

    def block(kj, runs, diagonal):
        start = pl.multiple_of(kj * tq, tq)
        new_runs = []
        for p in range(N_PAIRS):
            k = k_ref[pl.ds(start, tq), p * LANES:(p + 1) * LANES]
            vh = _head_values(v_ref[pl.ds(start, tq), p * LANES:(p + 1) * LANES])
            z = _dot_nt(qs_ref[p], k) * scale
            log_beta = jnp.minimum(z, 0.0) - jnp.log1p(jnp.exp(-jnp.abs(z)))
            log_keep = log_beta - z
            if diagonal:
                log_keep = jnp.where(strict, log_keep, 0.0)
            hi = log_keep.astype(BF16)
            lo = (log_keep - hi.astype(F32)).astype(BF16)
            suffix = _dot(hi, later) + _dot(lo, later)
            att = jnp.exp(log_beta + suffix + runs[p])
            if diagonal:
                att = jnp.where(strict, att, 0.0)
            att = att.astype(BF16)
            acc_ref[p] += jnp.concatenate([_dot(att[:tq], vh[0]), _dot(att[tq:], vh[1])], axis=0)
            new_runs.append(runs[p] + jnp.sum(log_keep, axis=1, keepdims=True))
        return tuple(new_runs)

    runs = block(qi, tuple(jnp.zeros((2 * tq, 1), F32) for _ in range(N_PAIRS)), True)
    lax.fori_loop(0, qi, lambda t, r: block(qi - 1 - t, r, False), runs)
    for p in range(N_PAIRS):
        o_ref[:, p * LANES:(p + 1) * LANES] = acc_ref[p, :tq] + acc_ref[p, tq:]


def _sb_attention(q, k, v, *, batch, seq):
    tq = ATT_BLOCK
    nq = seq // tq
    return pl.pallas_call(
        _sb_body,
        grid=(batch, nq),
        in_specs=[
            pl.BlockSpec((tq, HALF), lambda b, i: (b * nq + i, 0)),
            pl.BlockSpec((seq, HALF), lambda b, i: (b, 0)),
            pl.BlockSpec((seq, HALF), lambda b, i: (b, 0)),
        ],
        out_specs=pl.BlockSpec((tq, HALF), lambda b, i: (b * nq + i, 0)),
        out_shape=jax.ShapeDtypeStruct((batch * seq, HALF), F32),
        scratch_shapes=[pltpu.VMEM((N_PAIRS, 2 * tq, LANES), BF16), pltpu.VMEM((N_PAIRS, 2 * tq, LANES), F32)],
        compiler_params=_params("arbitrary", "arbitrary"),
        name="sb_attention",
    )(q, k, v)


def _mla_body(q_ref, k_ref, v_ref, o_ref, qs_ref, acc_ref):
    qi = pl.program_id(1)
    tq = ATT_BLOCK
    scale = (MLA_NOPE + MLA_ROPE) ** -0.5
    lane2 = lax.broadcasted_iota(jnp.int32, (tq, PAIR), 1)
    masks = [((lane2 >= hh * MLA_NOPE) & (lane2 < (hh + 1) * MLA_NOPE))
             | ((lane2 >= LANES + hh * MLA_ROPE) & (lane2 < LANES + (hh + 1) * MLA_ROPE)) for hh in range(2)]
    _stack_heads(q_ref, qs_ref, masks, PAIR)
    acc_ref[...] = jnp.zeros_like(acc_ref)
    row = lax.broadcasted_iota(jnp.int32, (2 * tq, tq), 0) & (tq - 1)
    col = lax.broadcasted_iota(jnp.int32, (2 * tq, tq), 1)
    causal = col <= row

    def block(kj, carry, diagonal):
        start = pl.multiple_of(kj * tq, tq)
        new = []
        for p in range(N_PAIRS):
            m, l = carry[p]
            k = k_ref[pl.ds(start, tq), p * PAIR:(p + 1) * PAIR]
            vh = _head_values(v_ref[pl.ds(start, tq), p * LANES:(p + 1) * LANES])
            s = _dot_nt(qs_ref[p], k) * scale
            if diagonal:
                s = jnp.where(causal, s, NEG_BIG)
            m_new = jnp.maximum(m, jnp.max(s, axis=1, keepdims=True))
            corr = jnp.exp(m - m_new)
            pexp = jnp.exp(s - m_new)
            l = l * corr + jnp.sum(pexp, axis=1, keepdims=True)
            pexp = pexp.astype(BF16)
            pv = jnp.concatenate([_dot(pexp[:tq], vh[0]), _dot(pexp[tq:], vh[1])], axis=0)
            acc_ref[p] = acc_ref[p] * corr + pv
            new.append((m_new, l))
        return tuple(new)

    init = tuple((jnp.full((2 * tq, 1), NEG_BIG, F32), jnp.zeros((2 * tq, 1), F32)) for _ in range(N_PAIRS))
    carry = block(qi, init, True)
    carry = lax.fori_loop(0, qi, lambda j, c: block(j, c, False), carry)
    for p in range(N_PAIRS):
        out = acc_ref[p] / carry[p][1]
        o_ref[:, p * LANES:(p + 1) * LANES] = out[:tq] + out[tq:]


def _mla_attention(q, k, v, *, batch, seq):
    tq = ATT_BLOCK
    nq = seq // tq
    return pl.pallas_call(
        _mla_body,
        grid=(batch, nq),
        in_specs=[
            pl.BlockSpec((tq, N_PAIRS * PAIR), lambda b, i: (b * nq + i, 0)),
            pl.BlockSpec((seq, N_PAIRS * PAIR), lambda b, i: (b, 0)),
            pl.BlockSpec((seq, HALF), lambda b, i: (b, 0)),
        ],
        out_specs=pl.BlockSpec((tq, HALF), lambda b, i: (b * nq + i, 0)),
        out_shape=jax.ShapeDtypeStruct((batch * seq, HALF), F32),
        scratch_shapes=[pltpu.VMEM((N_PAIRS, 2 * tq, PAIR), BF16), pltpu.VMEM((N_PAIRS, 2 * tq, LANES), F32)],
        compiler_params=_params("arbitrary", "arbitrary"),
        name="mla_attention",
    )(q, k, v)


def _row(v):
    return v.reshape(1, -1).astype(F32)


def _chain_layout(v):
    per_head = v.reshape(N_HEADS, HEAD_DIM).T
    return jnp.tile(per_head, (1, LANES // N_HEADS)).astype(F32)


def _rope_rotate_cols(w):
    half = MLA_ROPE // 2
    return jnp.concatenate([-w[..., half:], w[..., :half]], axis=-1)


def _layer0_mixer(x2, batch, seq, l0_w_in, rwkv_mix, rwkv_w0, rwkv_w2, rwkv_a0, rwkv_a2, rwkv_g2,
                  rwkv_k_k, rwkv_k_a, rwkv_r_k, rwkv_ln_g, rwkv_ln_b, ssm_conv_w, ssm_conv_b,
                  ssm_dt_bias, ssm_a_log, ssm_d, ssm_norm_g):
    w_r = l0_w_in[:, :RWKV_COLS].astype(BF16)
    zeros = jnp.zeros((RWKV_DECAY_LORA, HALF), F32)
    wl = jnp.concatenate([jnp.concatenate([rwkv_w2, zeros], axis=1),
                          jnp.concatenate([zeros, rwkv_a2], axis=1)], axis=0).astype(BF16)
    w0a0 = _row(jnp.concatenate([rwkv_w0, rwkv_a0]))
    rkvwa, gate = _rwkv_in(x2, w_r, _row(rwkv_mix), wl, w0a0, rwkv_g2.astype(BF16), seq=seq)
    xs = rkvwa.reshape(batch, seq, 5, N_HEADS, HEAD_DIM).transpose(1, 2, 4, 0, 3)
    xs = xs.reshape(seq, 5, HEAD_DIM, batch * N_HEADS)
    y = _rwkv_scan(xs, _chain_layout(rwkv_k_k), _chain_layout(rwkv_k_a), _chain_layout(rwkv_r_k.reshape(-1)),
                   _chain_layout(rwkv_ln_g), _chain_layout(rwkv_ln_b))
    y_a = y.reshape(seq, HEAD_DIM, batch, N_HEADS).transpose(2, 0, 3, 1).reshape(batch * seq, HALF)

    w_s = l0_w_in[:, RWKV_COLS:]
    pad = jnp.zeros((D_MODEL, LANES - N_HEADS), F32)
    w_s = jnp.concatenate([w_s[:, HALF:HALF + SSM_XBC], w_s[:, :HALF], w_s[:, HALF + SSM_XBC:], pad], axis=1)
    ps = _proj(x2, w_s.astype(BF16))
    dt_raw = ps[:, SSM_XBC + HALF:SSM_XBC + HALF + N_HEADS]
    dtt = dt_raw.reshape(batch, seq, N_HEADS).transpose(0, 2, 1)
    a_neg = -jnp.exp(ssm_a_log.astype(F32))
    lane_pad = jnp.zeros((LANES - N_HEADS,), F32)
    expand = jnp.concatenate([jnp.repeat(jnp.eye(N_HEADS, dtype=F32), HEAD_DIM, axis=1),
                              jnp.zeros((LANES - N_HEADS, HALF), F32)], axis=0)
    y_b = _ssd(ps, dtt, ssm_conv_w.astype(F32), _row(ssm_conv_b),
               _row(jnp.concatenate([ssm_dt_bias, lane_pad])), ssm_dt_bias.reshape(N_HEADS, 1).astype(F32),
               _row(jnp.concatenate([a_neg, lane_pad])), a_neg.reshape(N_HEADS, 1),
               expand, _row(jnp.repeat(ssm_d, HEAD_DIM)), _row(ssm_norm_g), batch=batch, seq=seq)
    return y_a, gate, y_b


def _layer1_mixer(h2, positions, batch, seq, l1_w_in, mla_q_norm_g, mla_w_uq, mla_kv_norm_g, mla_w_ukv):
    off = L1_SB + MLA_Q_LORA + MLA_KV_LORA
    w_kpe = l1_w_in[:, off:off + MLA_ROPE]
    lane_zeros = jnp.zeros((D_MODEL, LANES - 2 * MLA_ROPE), F32)
    w_kpe_blk = jnp.concatenate([w_kpe, w_kpe, lane_zeros], axis=1)
    w_kpe_rot = _rope_rotate_cols(w_kpe)
    w_kpe_rot_blk = jnp.concatenate([w_kpe_rot, w_kpe_rot, lane_zeros], axis=1)
    w_in = jnp.concatenate([l1_w_in[:, :off], w_kpe_blk, w_kpe_rot_blk], axis=1).astype(BF16)

    inv_freq = 1.0 / (ROPE_THETA ** (jnp.arange(0, MLA_ROPE, 2, dtype=F32) / MLA_ROPE))
    invf = jnp.concatenate([jnp.tile(inv_freq, 4), jnp.zeros((LANES - 2 * MLA_ROPE,), F32)]).reshape(1, LANES)

    wq = mla_w_uq.reshape(MLA_Q_LORA, N_HEADS, MLA_NOPE + MLA_ROPE)
    wq_nope = wq[:, :, :MLA_NOPE].reshape(MLA_Q_LORA, N_PAIRS, 2 * MLA_NOPE)
    wq_pe = wq[:, :, MLA_NOPE:]
    lz = jnp.zeros((MLA_Q_LORA, N_PAIRS, LANES - 2 * MLA_ROPE), F32)
    wq_pe_blk = jnp.concatenate([wq_pe.reshape(MLA_Q_LORA, N_PAIRS, 2 * MLA_ROPE), lz], axis=2)
    wq_rot_blk = jnp.concatenate([_rope_rotate_cols(wq_pe).reshape(MLA_Q_LORA, N_PAIRS, 2 * MLA_ROPE), lz], axis=2)
    wuq = jnp.concatenate([jnp.concatenate([wq_nope, wq_pe_blk], axis=2).reshape(MLA_Q_LORA, N_PAIRS * PAIR),
                           wq_rot_blk.reshape(MLA_Q_LORA, N_PAIRS * LANES)], axis=1).astype(BF16)

    wkv = mla_w_ukv.reshape(MLA_KV_LORA, N_HEADS, MLA_NOPE + MLA_V)
    wukv = jnp.concatenate([wkv[:, :, :MLA_NOPE].reshape(MLA_KV_LORA, HALF),
                            wkv[:, :, MLA_NOPE:].reshape(MLA_KV_LORA, HALF)], axis=1).astype(BF16)

    qsb, ksb, vsb, q, k, v = _l1_in(h2, positions.reshape(batch * seq, 1), w_in, invf,
                                    _row(mla_q_norm_g), wuq, _row(mla_kv_norm_g), wukv)
    y_c = _sb_attention(qsb, ksb, vsb, batch=batch, seq=seq)
    y_d = _mla_attention(q, k, v, batch=batch, seq=seq)
    return y_c, y_d


def _ffn_layer(h2, seq, w_up, conv_w, conv_b, w_down, ln_g, ln_b):
    return _ffn(h2, w_up[:, :D_FF].astype(BF16), w_up[:, D_FF:].astype(BF16), conv_w.astype(F32),
                _row(conv_b), w_down.astype(BF16), _row(ln_g), _row(ln_b), seq=seq)


def kernel(x, positions, l0_w_in, rwkv_mix, rwkv_w0, rwkv_w2, rwkv_a0, rwkv_a2, rwkv_g2, rwkv_k_k, rwkv_k_a, rwkv_r_k, rwkv_ln_g, rwkv_ln_b, ssm_conv_w, ssm_conv_b, ssm_dt_bias, ssm_a_log, ssm_d, ssm_norm_g, l0_w_out, l0_ln1_g, l0_ln1_b, ffn0_w_up, ffn0_conv_w, ffn0_conv_b, ffn0_w_down, l0_ln2_g, l0_ln2_b, l1_w_in, mla_q_norm_g, mla_w_uq, mla_kv_norm_g, mla_w_ukv, l1_w_out, l1_ln1_g, l1_ln1_b, ffn1_w_up, ffn1_conv_w, ffn1_conv_b, ffn1_w_down, l1_ln2_g, l1_ln2_b):
    batch, seq, _ = x.shape
    assert batch * N_HEADS == LANES, "the RWKV scan maps batch*heads onto the lane axis"
    x2 = x.reshape(batch * seq, D_MODEL).astype(F32)

    y_a, gate, y_b = _layer0_mixer(x2, batch, seq, l0_w_in, rwkv_mix, rwkv_w0, rwkv_w2, rwkv_a0, rwkv_a2,
                                   rwkv_g2, rwkv_k_k, rwkv_k_a, rwkv_r_k, rwkv_ln_g, rwkv_ln_b, ssm_conv_w,
                                   ssm_conv_b, ssm_dt_bias, ssm_a_log, ssm_d, ssm_norm_g)
    h = _mix_out(y_a, gate, y_b, x2, l0_w_out.astype(BF16), _row(l0_ln1_g), _row(l0_ln1_b))
    h = _ffn_layer(h, seq, ffn0_w_up, ffn0_conv_w, ffn0_conv_b, ffn0_w_down, l0_ln2_g, l0_ln2_b)

    y_c, y_d = _layer1_mixer(h, positions, batch, seq, l1_w_in, mla_q_norm_g, mla_w_uq, mla_kv_norm_g, mla_w_ukv)
    h = _mix_out(y_c, None, y_d, h, l1_w_out.astype(BF16), _row(l1_ln1_g), _row(l1_ln1_b))
    h = _ffn_layer(h, seq, ffn1_w_up, ffn1_conv_w, ffn1_conv_b, ffn1_w_down, l1_ln2_g, l1_ln2_b)
    return h.reshape(batch, seq, D_MODEL).astype(x.dtype)
```

```python
import functools
import math

import jax
import jax.numpy as jnp
from jax import lax
from jax.experimental import pallas as pl
from jax.experimental.pallas import tpu as pltpu

F32 = jnp.float32
BF16 = jnp.bfloat16

D_MODEL = 1024
HEAD_DIM = 64
N_HEADS = 8
HALF = N_HEADS * HEAD_DIM
RWKV_DECAY_LORA = 64
RWKV_A_LORA = 64
RWKV_GATE_LORA = 128
RWKV_GN_EPS = 64e-5
RWKV_COLS = 3 * HALF + RWKV_DECAY_LORA + RWKV_A_LORA + RWKV_GATE_LORA
SSM_GROUPS = 2
SSM_STATE = 128
SSM_CONV = 4
SSM_CHUNK = 128
SSM_XBC = HALF + 2 * SSM_GROUPS * SSM_STATE
SSM_COLS = HALF + SSM_XBC + N_HEADS
SSM_COLS_PAD = SSM_XBC + HALF + 128
MLA_NOPE = 64
MLA_ROPE = 32
MLA_V = 64
MLA_Q_LORA = 256
MLA_KV_LORA = 128
ROPE_THETA = 10000.0
D_FF = 2816
FFN_CONV = 3
DEPTH = 2
ALPHA = (2 * DEPTH) ** 0.25
LN_EPS = 1e-5

LANES = 128
SUBLANES = 8
VMEM_LIMIT = 56 * 1024 * 1024
NEG_BIG = -1e30

_HI = lax.Precision.HIGHEST


def _params(*sem):
    return pltpu.CompilerParams(dimension_semantics=sem, vmem_limit_bytes=VMEM_LIMIT)


def _dot(a, b, precision=None):
    return jnp.dot(a, b, preferred_element_type=F32, precision=precision)


def _dot_nt(a, b):
    return lax.dot_general(a, b, (((1,), (1,)), ((), ())), preferred_element_type=F32)


def _sigmoid(x):
    return 1.0 / (1.0 + jnp.exp(-x))


def _softplus(x):
    return jnp.maximum(x, 0.0) + jnp.log1p(jnp.exp(-jnp.abs(x)))


def _silu(x):
    return x * _sigmoid(x)


def _layer_norm(v, g, b):
    mu = jnp.mean(v, axis=-1, keepdims=True)
    d = v - mu
    var = jnp.mean(d * d, axis=-1, keepdims=True)
    return d * lax.rsqrt(var + LN_EPS) * g + b


def _const_spec(shape):
    nd = len(shape)
    return pl.BlockSpec(shape, lambda *_: (0,) * nd)


def _rwkv_in_body(x_ref, w_ref, mix_ref, wl_ref, w0a0_ref, g2_ref, r_ref, k_ref, v_ref, dec_ref, a_ref, g_ref,
                  carry_ref, *, tiles_per_seq):
    i = pl.program_id(0)
    p = _dot(x_ref[...].astype(BF16), w_ref[...])
    tm = p.shape[0]
    first = (i % tiles_per_seq) == 0
    prev_row = jnp.where(first, 0.0, carry_ref[SUBLANES - 1:SUBLANES, :])
    row = lax.broadcasted_iota(jnp.int32, p.shape, 0)
    prev = jnp.where(row == 0, prev_row, pltpu.roll(p, 1, 0))
    carry_ref[...] = p[tm - SUBLANES:tm, :]
    pm = p + (prev - p) * mix_ref[...]
    lo = pm[:, 3 * HALF:3 * HALF + LANES]
    lane = lax.broadcasted_iota(jnp.int32, lo.shape, 1)
    lo = jnp.where(lane < RWKV_DECAY_LORA, jnp.tanh(lo), lo)
    wa = _dot(lo.astype(BF16), wl_ref[...]) + w0a0_ref[...]
    log_w = -_softplus(-wa[:, :HALF]) - 0.5
    r_ref[...] = pm[:, 0:HALF]
    k_ref[...] = pm[:, HALF:2 * HALF]
    v_ref[...] = pm[:, 2 * HALF:3 * HALF]
    dec_ref[...] = jnp.exp(-jnp.exp(log_w))
    a_ref[...] = _sigmoid(wa[:, HALF:])
    g_lo = _sigmoid(pm[:, 3 * HALF + LANES:])
    g_ref[...] = _dot(g_lo.astype(BF16), g2_ref[...])


def _rwkv_in(x2, w_r, mix, wl, w0a0, g2, *, seq, tm=256):
    m = x2.shape[0]
    tps = seq // tm
    batch = m // seq
    body = functools.partial(_rwkv_in_body, tiles_per_seq=tps)
    time_major = pl.BlockSpec((tm, HALF), lambda i: (i % tps, i // tps))
    return pl.pallas_call(
        body,
        grid=(m // tm,),
        in_specs=[
            pl.BlockSpec((tm, D_MODEL), lambda i: (i, 0)),
            _const_spec(w_r.shape), _const_spec(mix.shape), _const_spec(wl.shape),
            _const_spec(w0a0.shape), _const_spec(g2.shape),
        ],
        out_specs=[time_major] * 5 + [pl.BlockSpec((tm, HALF), lambda i: (i, 0))],
        out_shape=[jax.ShapeDtypeStruct((seq, batch * HALF), F32)] * 5 + [jax.ShapeDtypeStruct((m, HALF), F32)],
        scratch_shapes=[pltpu.VMEM((SUBLANES, RWKV_COLS), F32)],
        compiler_params=_params("arbitrary"),
        name="rwkv_in",
    )(x2, w_r, mix, wl, w0a0, g2)


def _proj_body(x_ref, w_ref, o_ref):
    o_ref[...] = _dot(x_ref[...].astype(BF16), w_ref[...])


def _proj(x2, w, *, tm=512):
    m, k = x2.shape
    n = w.shape[1]
    return pl.pallas_call(
        _proj_body,
        grid=(m // tm,),
        in_specs=[pl.BlockSpec((tm, k), lambda i: (i, 0)), _const_spec(w.shape)],
        out_specs=pl.BlockSpec((tm, n), lambda i: (i, 0)),
        out_shape=jax.ShapeDtypeStruct((m, n), F32),
        compiler_params=_params("arbitrary"),
        name="proj",
    )(x2, w)


SCAN_UNROLL = 16


def _to_chain_lanes(z):
    zt = z.T
    return jnp.concatenate([zt[:HEAD_DIM], zt[HEAD_DIM:]], axis=1)


def _from_chain_lanes(y):
    return jnp.concatenate([y[:, :LANES // 2], y[:, LANES // 2:]], axis=0).T


def _rwkv_scan_body(r_ref, k_ref, v_ref, w_ref, a_ref, kk_ref, ka_ref, rk_ref, lng_ref, lnb_ref, o_ref,
                    h_ref, rs_ref, ws_ref, vs_ref, av_ref, bv_ref, k2_ref, bon_ref, ys_ref, *, tb):
    @pl.when(pl.program_id(0) == 0)
    def _():
        h_ref[...] = jnp.zeros_like(h_ref)

    for t in range(tb):
        r = _to_chain_lanes(r_ref[t])
        k = _to_chain_lanes(k_ref[t])
        v = _to_chain_lanes(v_ref[t])
        a = _to_chain_lanes(a_ref[t])
        kk = k * kk_ref[...]
        nrm = jnp.sqrt(jnp.sum(kk * kk, axis=0, keepdims=True))
        kk = kk / jnp.maximum(nrm, 1e-12)
        k2 = k * (1.0 + (a - 1.0) * ka_ref[...])
        rs_ref[t] = r
        vs_ref[t] = v
        ws_ref[t] = _to_chain_lanes(w_ref[t])
        av_ref[t] = -kk
        bv_ref[t] = kk * a
        k2_ref[t] = k2
        bon_ref[t] = jnp.sum(r * k2 * rk_ref[...], axis=0, keepdims=True) * v

    def step(t, carry):
        zero = jnp.zeros((HEAD_DIM, LANES), F32)

        def read(j, u):
            return u + h_ref[j] * av_ref[t, pl.ds(j, 1), :]

        u = lax.fori_loop(0, HEAD_DIM, read, zero, unroll=SCAN_UNROLL)
        vt = vs_ref[t]

        def update(j, y):
            wj = ws_ref[t, pl.ds(j, 1), :]
            rj = rs_ref[t, pl.ds(j, 1), :]
            bj = bv_ref[t, pl.ds(j, 1), :]
            kj = k2_ref[t, pl.ds(j, 1), :]
            hn = h_ref[j] * wj + u * bj + vt * kj
            h_ref[j] = hn
            return y + hn * rj

        ys_ref[t] = lax.fori_loop(0, HEAD_DIM, update, zero, unroll=SCAN_UNROLL)
        return carry

    lax.fori_loop(0, tb, step, 0)

    for t in range(tb):
        y = ys_ref[t]
        mu = jnp.mean(y, axis=0, keepdims=True)
        d = y - mu
        var = jnp.mean(d * d, axis=0, keepdims=True)
        y = d * lax.rsqrt(var + RWKV_GN_EPS) * lng_ref[...] + lnb_ref[...] + bon_ref[t]
        o_ref[t] = _from_chain_lanes(y)


def _rwkv_scan(r, k, v, w, a, kk, ka, rk, lng, lnb, *, tb=16):
    t = r.shape[0]
    vec = pl.BlockSpec((HEAD_DIM, LANES), lambda i: (0, 0))
    blk = (tb, HEAD_DIM, LANES)
    seq_spec = pl.BlockSpec(blk, lambda i: (i, 0, 0))
    return pl.pallas_call(
        functools.partial(_rwkv_scan_body, tb=tb),
        grid=(t // tb,),
        in_specs=[seq_spec] * 5 + [vec] * 5,
        out_specs=seq_spec,
        out_shape=jax.ShapeDtypeStruct((t, HEAD_DIM, LANES), F32),
        scratch_shapes=[pltpu.VMEM((HEAD_DIM, HEAD_DIM, LANES), F32)] + [pltpu.VMEM(blk, F32)] * 8,
        compiler_params=_params("arbitrary"),
        name="rwkv_scan",
    )(r, k, v, w, a, kk, ka, rk, lng, lnb)


def _ssd_body(xbc_ref, z_ref, dt_ref, dtt_ref, cw_ref, cb_ref, dtb_row_ref, dtb_col_ref,
              a_row_ref, a_col_ref, e_ref, dsk_ref, ng_ref, o_ref, ext_ref, st_ref):
    L = SSM_CHUNK

    @pl.when(pl.program_id(1) == 0)
    def _():
        ext_ref[0:SUBLANES, :] = jnp.zeros((SUBLANES, SSM_XBC), F32)
        st_ref[...] = jnp.zeros_like(st_ref)

    ext_ref[SUBLANES:SUBLANES + L, :] = xbc_ref[...]
    conv = cb_ref[...]
    for i in range(SSM_CONV):
        off = SUBLANES - (SSM_CONV - 1) + i
        conv = conv + cw_ref[i:i + 1, :] * ext_ref[off:off + L, :]
    ext_ref[0:SUBLANES, :] = ext_ref[L:L + SUBLANES, :]
    xc = _silu(conv)
    xs = xc[:, :HALF]
    bm = xc[:, HALF:HALF + SSM_GROUPS * SSM_STATE]
    cm = xc[:, HALF + SSM_GROUPS * SSM_STATE:]

    row = lax.broadcasted_iota(jnp.int32, (L, L), 0)
    col = lax.broadcasted_iota(jnp.int32, (L, L), 1)
    lower = row >= col
    dtc = _softplus(dt_ref[...] + dtb_row_ref[...])
    cum_col = _dot(lower.astype(F32), dtc * a_row_ref[...], _HI)
    dtr = _softplus(dtt_ref[0] + dtb_col_ref[...])
    cum_row = _dot(dtr * a_col_ref[...], (row <= col).astype(F32), _HI)
    dtx = _dot(dtc, e_ref[...], _HI)
    ccx = _dot(cum_col, e_ref[...], _HI)
    x = xs * dtx
    ecc = jnp.exp(ccx)
    clast = ccx[L - 1:L, :]
    xd = x * jnp.exp(clast - ccx)
    elast = jnp.exp(clast)
    lane = lax.broadcasted_iota(jnp.int32, (L, LANES), 1)

    ys = []
    for g in range(SSM_GROUPS):
        bg = bm[:, g * SSM_STATE:(g + 1) * SSM_STATE]
        cg = cm[:, g * SSM_STATE:(g + 1) * SSM_STATE].astype(BF16)
        cbm = _dot_nt(cg, bg.astype(BF16))
        bgt = bg.T.astype(BF16)
        for q in range(2):
            p = g * 2 + q
            sl = slice(p * LANES, (p + 1) * LANES)
            st = st_ref[p]
            yp = _dot(cg, st.astype(BF16)) * ecc[:, sl]
            xp = x[:, sl]
            for hh in range(2):
                h = 2 * p + hh
                seg = cum_col[:, h:h + 1] - cum_row[h:h + 1, :]
                dec = jnp.exp(jnp.where(lower, seg, NEG_BIG))
                xh = jnp.where((lane // HEAD_DIM) == hh, xp, 0.0).astype(BF16)
                yp = yp + _dot((cbm * dec).astype(BF16), xh)
            st_ref[p] = st * elast[:, sl] + _dot(bgt, xd[:, sl].astype(BF16))
            ys.append(yp)
    y = jnp.concatenate(ys, axis=1) + xs * dsk_ref[...]
    u = y * _silu(z_ref[...])
    gw = HALF // SSM_GROUPS
    outs = []
    for g in range(SSM_GROUPS):
        ug = u[:, g * gw:(g + 1) * gw]
        outs.append(ug * lax.rsqrt(jnp.mean(ug * ug, axis=-1, keepdims=True) + 1e-5))
    o_ref[...] = jnp.concatenate(outs, axis=1) * ng_ref[...]


def _ssd(ps, dtt, cw, cb, dtb_row, dtb_col, a_row, a_col, e, dsk, ng, *, batch, seq):
    L = SSM_CHUNK
    nc = seq // L
    consts = [cw, cb, dtb_row, dtb_col, a_row, a_col, e, dsk, ng]
    return pl.pallas_call(
        _ssd_body,
        grid=(batch, nc),
        in_specs=[
            pl.BlockSpec((L, SSM_XBC), lambda b, c: (b * nc + c, 0)),
            pl.BlockSpec((L, HALF), lambda b, c: (b * nc + c, SSM_XBC // HALF)),
            pl.BlockSpec((L, LANES), lambda b, c: (b * nc + c, (SSM_XBC + HALF) // LANES)),
            pl.BlockSpec((1, N_HEADS, L), lambda b, c: (b, 0, c)),
        ] + [_const_spec(c.shape) for c in consts],
        out_specs=pl.BlockSpec((L, HALF), lambda b, c: (b * nc + c, 0)),
        out_shape=jax.ShapeDtypeStruct((batch * seq, HALF), F32),
        scratch_shapes=[pltpu.VMEM((L + 2 * SUBLANES, SSM_XBC), F32),
                        pltpu.VMEM((N_HEADS // 2, SSM_STATE, LANES), F32)],
        compiler_params=_params("arbitrary", "arbitrary"),
        name="ssd",
    )(ps, ps, ps, dtt, *consts)


def _mix_out_body(*refs, gated):
    if gated:
        ya_ref, g_ref, yb_ref, x_ref, w_ref, lng_ref, lnb_ref, o_ref = refs
        ya = ya_ref[...] * g_ref[...]
    else:
        ya_ref, yb_ref, x_ref, w_ref, lng_ref, lnb_ref, o_ref = refs
        ya = ya_ref[...]
    mixed = _dot(ya.astype(BF16), w_ref[0:HALF, :]) + _dot(yb_ref[...].astype(BF16), w_ref[HALF:, :])
    o_ref[...] = _layer_norm(ALPHA * x_ref[...] + mixed, lng_ref[...], lnb_ref[...])


def _mix_out(ya, g, yb, x2, w, lng, lnb, *, seq, tm=512):
    m = x2.shape[0]
    tm = min(tm, seq)
    tps = seq // tm
    half = pl.BlockSpec((tm, HALF), lambda i: (i, 0))
    full = pl.BlockSpec((tm, D_MODEL), lambda i: (i, 0))
    gated = g is not None
    acts = [ya, g, yb] if gated else [ya, yb]
    act_specs = [pl.BlockSpec((tm, HALF), lambda i: (i % tps, i // tps)), half, half] if gated else [half, half]
    return pl.pallas_call(
        functools.partial(_mix_out_body, gated=gated),
        grid=(m // tm,),
        in_specs=act_specs + [full, _const_spec(w.shape), _const_spec(lng.shape), _const_spec(lnb.shape)],
        out_specs=full,
        out_shape=jax.ShapeDtypeStruct((m, D_MODEL), F32),
        compiler_params=_params("arbitrary"),
        name="mix_out",
    )(*acts, x2, w, lng, lnb)


def _shift_rows(g, s, tail):
    rolled = pltpu.roll(g, s, 0)
    head = rolled[:SUBLANES]
    row = lax.broadcasted_iota(jnp.int32, head.shape, 0)
    for r in range(s):
        head = jnp.where(row == r, tail[SUBLANES - s + r:SUBLANES - s + r + 1, :], head)
    return jnp.concatenate([head, rolled[SUBLANES:]], axis=0)


def _ffn_body(x_ref, wg_ref, wu_ref, cw_ref, cb_ref, wd_ref, lng_ref, lnb_ref, o_ref,
              xb_ref, acc_ref, act_ref, carry_ref, *, tiles_per_seq, chunk):
    i = pl.program_id(0)
    f = pl.program_id(1)
    nf = pl.num_programs(1)
    tm, tf = act_ref.shape

    @pl.when(f == 0)
    def _():
        xb_ref[...] = x_ref[...].astype(BF16)

    xb = xb_ref[...]
    first = (i % tiles_per_seq) == 0
    for c0 in range(0, tf, chunk):
        cs = slice(c0, min(c0 + chunk, tf))
        gate = _dot(xb, wg_ref[:, cs])
        up = _dot(xb, wu_ref[:, cs])
        tail = jnp.where(first, 0.0, carry_ref[f, :, cs])
        carry_ref[f, :, cs] = gate[tm - SUBLANES:tm, :]
        conv = cb_ref[:, cs] + cw_ref[FFN_CONV - 1:FFN_CONV, cs] * gate
        for s in range(1, FFN_CONV):
            conv = conv + cw_ref[FFN_CONV - 1 - s:FFN_CONV - s, cs] * _shift_rows(gate, s, tail)
        act_ref[:, cs] = (_silu(conv) * up).astype(BF16)
    down = _dot(act_ref[...], wd_ref[...])

    @pl.when(f == 0)
    def _():
        acc_ref[...] = down

    @pl.when(f == nf - 1)
    def _():
        o_ref[...] = _layer_norm(ALPHA * x_ref[...] + acc_ref[...] + down, lng_ref[...], lnb_ref[...])


def _ffn(x2, wg, wu, cw, cb, wd, lng, lnb, *, seq, tm=512, tf=D_FF // 2, chunk=256):
    m = x2.shape[0]
    tm = min(tm, seq)
    nf = D_FF // tf
    assert nf == 2, "the accumulator is written on the first d_ff tile and consumed on the last"
    return pl.pallas_call(
        functools.partial(_ffn_body, tiles_per_seq=seq // tm, chunk=chunk),
        grid=(m // tm, nf),
        in_specs=[
            pl.BlockSpec((tm, D_MODEL), lambda i, f: (i, 0)),
            pl.BlockSpec((D_MODEL, tf), lambda i, f: (0, f)),
            pl.BlockSpec((D_MODEL, tf), lambda i, f: (0, f)),
            pl.BlockSpec((FFN_CONV, tf), lambda i, f: (0, f)),
            pl.BlockSpec((1, tf), lambda i, f: (0, f)),
            pl.BlockSpec((tf, D_MODEL), lambda i, f: (f, 0)),
            _const_spec(lng.shape), _const_spec(lnb.shape),
        ],
        out_specs=pl.BlockSpec((tm, D_MODEL), lambda i, f: (i, 0)),
        out_shape=jax.ShapeDtypeStruct((m, D_MODEL), F32),
        scratch_shapes=[pltpu.VMEM((tm, D_MODEL), BF16), pltpu.VMEM((tm, D_MODEL), F32),
                        pltpu.VMEM((tm, tf), BF16), pltpu.VMEM((nf, SUBLANES, tf), F32)],
        compiler_params=_params("arbitrary", "arbitrary"),
        name="conv_ffn",
    )(x2, wg, wu, cw, cb, wd, lng, lnb)


SB_SCALE = HEAD_DIM ** -0.5
assert math.frexp(SB_SCALE)[0] == 0.5, "folded into q before the bf16 cast, so it must be a power of two"
PAIR = 2 * LANES
N_PAIRS = N_HEADS // 2
L1_SB = 3 * HALF
L1_COLS_PAD = L1_SB + MLA_Q_LORA + MLA_KV_LORA + 2 * LANES


def _l1_in_body(x_ref, pos_ref, w_ref, invf_ref, qg_ref, wuq_ref, kvg_ref, wukv_ref,
                qsb_ref, ksb_ref, vsb_ref, q_ref, k_ref, v_ref):
    p = _dot(x_ref[...].astype(BF16), w_ref[...])
    qsb_ref[...] = (p[:, 0:HALF] * SB_SCALE).astype(BF16)
    ksb_ref[...] = p[:, HALF:2 * HALF].astype(BF16)
    vsb_ref[...] = p[:, 2 * HALF:3 * HALF].astype(BF16)
    c_q = p[:, L1_SB:L1_SB + MLA_Q_LORA]
    c_kv = p[:, L1_SB + MLA_Q_LORA:L1_SB + MLA_Q_LORA + MLA_KV_LORA]
    off = L1_SB + MLA_Q_LORA + MLA_KV_LORA
    kpe = p[:, off:off + LANES]
    kpe_rot = p[:, off + LANES:off + 2 * LANES]
    ang = pos_ref[...].astype(F32) * invf_ref[...]
    cos = jnp.cos(ang)
    sin = jnp.sin(ang)
    kpe = (kpe * cos + kpe_rot * sin).astype(BF16)
    cqn = c_q * lax.rsqrt(jnp.mean(c_q * c_q, axis=-1, keepdims=True) + 1e-6) * qg_ref[...]
    q = _dot(cqn.astype(BF16), wuq_ref[...])
    ckn = c_kv * lax.rsqrt(jnp.mean(c_kv * c_kv, axis=-1, keepdims=True) + 1e-6) * kvg_ref[...]
    kv = _dot(ckn.astype(BF16), wukv_ref[...])
    for pr in range(N_PAIRS):
        q_ref[:, pr * PAIR:pr * PAIR + LANES] = q[:, pr * PAIR:pr * PAIR + LANES].astype(BF16)
        q_pe = q[:, pr * PAIR + LANES:(pr + 1) * PAIR]
        q_rot = q[:, N_PAIRS * PAIR + pr * LANES:N_PAIRS * PAIR + (pr + 1) * LANES]
        q_ref[:, pr * PAIR + LANES:(pr + 1) * PAIR] = (q_pe * cos + q_rot * sin).astype(BF16)
        k_ref[:, pr * PAIR:pr * PAIR + LANES] = kv[:, pr * LANES:(pr + 1) * LANES].astype(BF16)
        k_ref[:, pr * PAIR + LANES:(pr + 1) * PAIR] = kpe
    v_ref[...] = kv[:, HALF:].astype(BF16)


def _l1_in(x2, pos2, w, invf, qg, wuq, kvg, wukv, *, tm=512):
    m = x2.shape[0]
    consts = [w, invf, qg, wuq, kvg, wukv]
    half = pl.BlockSpec((tm, HALF), lambda i: (i, 0))
    cat = pl.BlockSpec((tm, N_PAIRS * PAIR), lambda i: (i, 0))
    sd = jax.ShapeDtypeStruct
    return pl.pallas_call(
        _l1_in_body,
        grid=(m // tm,),
        in_specs=[pl.BlockSpec((tm, D_MODEL), lambda i: (i, 0)), pl.BlockSpec((tm, 1), lambda i: (i, 0))]
        + [_const_spec(c.shape) for c in consts],
        out_specs=[half, half, half, cat, cat, half],
        out_shape=[sd((m, HALF), BF16)] * 3 + [sd((m, N_PAIRS * PAIR), BF16)] * 2 + [sd((m, HALF), BF16)],
        compiler_params=_params("arbitrary"),
        name="l1_in",
    )(x2, pos2, *consts)


ATT_BLOCK = 256


def _stack_heads(q_ref, qs_ref, masks, width):
    tq = ATT_BLOCK
    for p in range(N_PAIRS):
        qp = q_ref[:, p * width:(p + 1) * width]
        for hh in range(2):
            qs_ref[p, hh * tq:(hh + 1) * tq, :] = jnp.where(masks[hh], qp, jnp.zeros_like(qp))


def _head_values(v):
    lane = lax.broadcasted_iota(jnp.int32, v.shape, 1)
    return [jnp.where((lane // HEAD_DIM) == hh, v, jnp.zeros_like(v)) for hh in range(2)]


def _sb_body(q_ref, k_ref, v_ref, o_ref, qs_ref, acc_ref):
    qi = pl.program_id(1)
    tq = ATT_BLOCK
    lane = lax.broadcasted_iota(jnp.int32, (tq, LANES), 1)
    _stack_heads(q_ref, qs_ref, [(lane // HEAD_DIM) == hh for hh in range(2)], LANES)
    acc_ref[...] = jnp.zeros_like(acc_ref)
    row = lax.broadcasted_iota(jnp.int32, (2 * tq, tq), 0) & (tq - 1)
    col = lax.broadcasted_iota(jnp.int32, (2 * tq, tq), 1)
    strict = col < row
    later = ((lax.broadcasted_iota(jnp.int32, (2 * tq, tq), 0) & (tq - 1))
             > lax.broadcasted_iota(jnp.int32, (2 * tq, tq), 1)).astype(BF16)

    def block(kj, runs, diagonal):
        start = pl.multiple_of(kj * tq, tq)
        new_runs = []
        for p in range(N_PAIRS):
            k = k_ref[pl.ds(start, tq), p * LANES:(p + 1) * LANES]
            vh = _head_values(v_ref[pl.ds(start, tq), p * LANES:(p + 1) * LANES])
            z = _dot_nt(qs_ref[p], k)
            log_beta = jnp.minimum(z, 0.0) - jnp.log(1.0 + jnp.exp(-jnp.abs(z)))
            log_keep = log_beta - z
            if diagonal:
                log_keep = jnp.where(strict, log_keep, 0.0)
            hi = log_keep.astype(BF16)
            lo = (log_keep - hi.astype(F32)).astype(BF16)
            suffix = _dot(jnp.concatenate([hi, lo], axis=1), later)
            att = jnp.exp(log_beta + suffix + runs[p])
            if diagonal:
                att = jnp.where(strict, att, 0.0)
            att = att.astype(BF16)
            acc_ref[p] += jnp.concatenate([_dot(att[:tq], vh[0]), _dot(att[tq:], vh[1])], axis=0)
            new_runs.append(runs[p] + jnp.sum(log_keep, axis=1, keepdims=True))
        return tuple(new_runs)

    runs = block(qi, tuple(jnp.zeros((2 * tq, 1), F32) for _ in range(N_PAIRS)), True)
    lax.fori_loop(0, qi, lambda t, r: block(qi - 1 - t, r, False), runs)
    for p in range(N_PAIRS):
        o_ref[:, p * LANES:(p + 1) * LANES] = acc_ref[p, :tq] + acc_ref[p, tq:]


def _sb_attention(q, k, v, *, batch, seq):
    tq = ATT_BLOCK
    nq = seq // tq
    return pl.pallas_call(
        _sb_body,
        grid=(batch, nq),
        in_specs=[
            pl.BlockSpec((tq, HALF), lambda b, i: (b * nq + i, 0)),
            pl.BlockSpec((seq, HALF), lambda b, i: (b, 0)),
            pl.BlockSpec((seq, HALF), lambda b, i: (b, 0)),
        ],
        out_specs=pl.BlockSpec((tq, HALF), lambda b, i: (b * nq + i, 0)),
        out_shape=jax.ShapeDtypeStruct((batch * seq, HALF), F32),
        scratch_shapes=[pltpu.VMEM((N_PAIRS, 2 * tq, LANES), BF16), pltpu.VMEM((N_PAIRS, 2 * tq, LANES), F32)],
        compiler_params=_params("arbitrary", "arbitrary"),
        name="sb_attention",
    )(q, k, v)


def _mla_body(q_ref, k_ref, v_ref, o_ref, qs_ref, acc_ref):
    qi = pl.program_id(1)
    tq = ATT_BLOCK
    scale = (MLA_NOPE + MLA_ROPE) ** -0.5
    lane2 = lax.broadcasted_iota(jnp.int32, (tq, PAIR), 1)
    masks = [((lane2 >= hh * MLA_NOPE) & (lane2 < (hh + 1) * MLA_NOPE))
             | ((lane2 >= LANES + hh * MLA_ROPE) & (lane2 < LANES + (hh + 1) * MLA_ROPE)) for hh in range(2)]
    _stack_heads(q_ref, qs_ref, masks, PAIR)
    acc_ref[...] = jnp.zeros_like(acc_ref)
    row = lax.broadcasted_iota(jnp.int32, (2 * tq, tq), 0) & (tq - 1)
    col = lax.broadcasted_iota(jnp.int32, (2 * tq, tq), 1)
    causal = col <= row

    def block(kj, carry, diagonal):
        start = pl.multiple_of(kj * tq, tq)
        new = []
        for p in range(N_PAIRS):
            m, l = carry[p]
            k = k_ref[pl.ds(start, tq), p * PAIR:(p + 1) * PAIR]
            vh = _head_values(v_ref[pl.ds(start, tq), p * LANES:(p + 1) * LANES])
            s = _dot_nt(qs_ref[p], k) * scale
            if diagonal:
                s = jnp.where(causal, s, NEG_BIG)
            m_new = jnp.maximum(m, jnp.max(s, axis=1, keepdims=True))
            corr = jnp.exp(m - m_new)
            pexp = jnp.exp(s - m_new)
            l = l * corr + jnp.sum(pexp, axis=1, keepdims=True)
            pexp = pexp.astype(BF16)
            pv = jnp.concatenate([_dot(pexp[:tq], vh[0]), _dot(pexp[tq:], vh[1])], axis=0)
            acc_ref[p] = acc_ref[p] * corr + pv
            new.append((m_new, l))
        return tuple(new)

    init = tuple((jnp.full((2 * tq, 1), NEG_BIG, F32), jnp.zeros((2 * tq, 1), F32)) for _ in range(N_PAIRS))
    carry = block(qi, init, True)
    carry = lax.fori_loop(0, qi, lambda j, c: block(j, c, False), carry)
    for p in range(N_PAIRS):
        out = acc_ref[p] / carry[p][1]
        o_ref[:, p * LANES:(p + 1) * LANES] = out[:tq] + out[tq:]


def _mla_attention(q, k, v, *, batch, seq):
    tq = ATT_BLOCK
    nq = seq // tq
    return pl.pallas_call(
        _mla_body,
        grid=(batch, nq),
        in_specs=[
            pl.BlockSpec((tq, N_PAIRS * PAIR), lambda b, i: (b * nq + i, 0)),
            pl.BlockSpec((seq, N_PAIRS * PAIR), lambda b, i: (b, 0)),
            pl.BlockSpec((seq, HALF), lambda b, i: (b, 0)),
        ],
        out_specs=pl.BlockSpec((tq, HALF), lambda b, i: (b * nq + i, 0)),
        out_shape=jax.ShapeDtypeStruct((batch * seq, HALF), F32),
        scratch_shapes=[pltpu.VMEM((N_PAIRS, 2 * tq, PAIR), BF16), pltpu.VMEM((N_PAIRS, 2 * tq, LANES), F32)],
        compiler_params=_params("arbitrary", "arbitrary"),
        name="mla_attention",
    )(q, k, v)


def _row(v):
    return v.reshape(1, -1).astype(F32)


def _chain_layout(v):
    lane = jnp.arange(LANES)
    head = (2 * (lane % (LANES // 2)) + lane // (LANES // 2)) % N_HEADS
    return v.reshape(N_HEADS, HEAD_DIM).T[:, head].astype(F32)


def _rope_rotate_cols(w):
    half = MLA_ROPE // 2
    return jnp.concatenate([-w[..., half:], w[..., :half]], axis=-1)


def _layer0_mixer(x2, batch, seq, l0_w_in, rwkv_mix, rwkv_w0, rwkv_w2, rwkv_a0, rwkv_a2, rwkv_g2,
                  rwkv_k_k, rwkv_k_a, rwkv_r_k, rwkv_ln_g, rwkv_ln_b, ssm_conv_w, ssm_conv_b,
                  ssm_dt_bias, ssm_a_log, ssm_d, ssm_norm_g):
    w_r = l0_w_in[:, :RWKV_COLS].astype(BF16)
    zeros = jnp.zeros((RWKV_DECAY_LORA, HALF), F32)
    wl = jnp.concatenate([jnp.concatenate([rwkv_w2, zeros], axis=1),
                          jnp.concatenate([zeros, rwkv_a2], axis=1)], axis=0).astype(BF16)
    w0a0 = _row(jnp.concatenate([rwkv_w0, rwkv_a0]))
    *rkvwa, gate = _rwkv_in(x2, w_r, _row(rwkv_mix), wl, w0a0, rwkv_g2.astype(BF16), seq=seq)
    as_tiles = lambda t: t.reshape(seq, HEAD_DIM, LANES)
    y = _rwkv_scan(*[as_tiles(t) for t in rkvwa], _chain_layout(rwkv_k_k), _chain_layout(rwkv_k_a),
                   _chain_layout(rwkv_r_k.reshape(-1)), _chain_layout(rwkv_ln_g), _chain_layout(rwkv_ln_b))
    y_a = y.reshape(seq, batch * HALF)

    w_s = l0_w_in[:, RWKV_COLS:]
    pad = jnp.zeros((D_MODEL, LANES - N_HEADS), F32)
    w_s = jnp.concatenate([w_s[:, HALF:HALF + SSM_XBC], w_s[:, :HALF], w_s[:, HALF + SSM_XBC:], pad], axis=1)
    ps = _proj(x2, w_s.astype(BF16))
    dt_raw = ps[:, SSM_XBC + HALF:SSM_XBC + HALF + N_HEADS]
    dtt = dt_raw.reshape(batch, seq, N_HEADS).transpose(0, 2, 1)
    a_neg = -jnp.exp(ssm_a_log.astype(F32))
    lane_pad = jnp.zeros((LANES - N_HEADS,), F32)
    expand = jnp.concatenate([jnp.repeat(jnp.eye(N_HEADS, dtype=F32), HEAD_DIM, axis=1),
                              jnp.zeros((LANES - N_HEADS, HALF), F32)], axis=0)
    y_b = _ssd(ps, dtt, ssm_conv_w.astype(F32), _row(ssm_conv_b),
               _row(jnp.concatenate([ssm_dt_bias, lane_pad])), ssm_dt_bias.reshape(N_HEADS, 1).astype(F32),
               _row(jnp.concatenate([a_neg, lane_pad])), a_neg.reshape(N_HEADS, 1),
               expand, _row(jnp.repeat(ssm_d, HEAD_DIM)), _row(ssm_norm_g), batch=batch, seq=seq)
    return y_a, gate, y_b


def _layer1_mixer(h2, positions, batch, seq, l1_w_in, mla_q_norm_g, mla_w_uq, mla_kv_norm_g, mla_w_ukv):
    off = L1_SB + MLA_Q_LORA + MLA_KV_LORA
    w_kpe = l1_w_in[:, off:off + MLA_ROPE]
    lane_zeros = jnp.zeros((D_MODEL, LANES - 2 * MLA_ROPE), F32)
    w_kpe_blk = jnp.concatenate([w_kpe, w_kpe, lane_zeros], axis=1)
    w_kpe_rot = _rope_rotate_cols(w_kpe)
    w_kpe_rot_blk = jnp.concatenate([w_kpe_rot, w_kpe_rot, lane_zeros], axis=1)
    w_in = jnp.concatenate([l1_w_in[:, :off], w_kpe_blk, w_kpe_rot_blk], axis=1).astype(BF16)

    inv_freq = 1.0 / (ROPE_THETA ** (jnp.arange(0, MLA_ROPE, 2, dtype=F32) / MLA_ROPE))
    invf = jnp.concatenate([jnp.tile(inv_freq, 4), jnp.zeros((LANES - 2 * MLA_ROPE,), F32)]).reshape(1, LANES)

    wq = mla_w_uq.reshape(MLA_Q_LORA, N_HEADS, MLA_NOPE + MLA_ROPE)
    wq_nope = wq[:, :, :MLA_NOPE].reshape(MLA_Q_LORA, N_PAIRS, 2 * MLA_NOPE)
    wq_pe = wq[:, :, MLA_NOPE:]
    lz = jnp.zeros((MLA_Q_LORA, N_PAIRS, LANES - 2 * MLA_ROPE), F32)
    wq_pe_blk = jnp.concatenate([wq_pe.reshape(MLA_Q_LORA, N_PAIRS, 2 * MLA_ROPE), lz], axis=2)
    wq_rot_blk = jnp.concatenate([_rope_rotate_cols(wq_pe).reshape(MLA_Q_LORA, N_PAIRS, 2 * MLA_ROPE), lz], axis=2)
    wuq = jnp.concatenate([jnp.concatenate([wq_nope, wq_pe_blk], axis=2).reshape(MLA_Q_LORA, N_PAIRS * PAIR),
                           wq_rot_blk.reshape(MLA_Q_LORA, N_PAIRS * LANES)], axis=1).astype(BF16)

    wkv = mla_w_ukv.reshape(MLA_KV_LORA, N_HEADS, MLA_NOPE + MLA_V)
    wukv = jnp.concatenate([wkv[:, :, :MLA_NOPE].reshape(MLA_KV_LORA, HALF),
                            wkv[:, :, MLA_NOPE:].reshape(MLA_KV_LORA, HALF)], axis=1).astype(BF16)

    qsb, ksb, vsb, q, k, v = _l1_in(h2, positions.reshape(batch * seq, 1), w_in, invf,
                                    _row(mla_q_norm_g), wuq, _row(mla_kv_norm_g), wukv)
    y_c = _sb_attention(qsb, ksb, vsb, batch=batch, seq=seq)
    y_d = _mla_attention(q, k, v, batch=batch, seq=seq)
    return y_c, y_d


def _ffn_layer(h2, seq, w_up, conv_w, conv_b, w_down, ln_g, ln_b):
    return _ffn(h2, w_up[:, :D_FF].astype(BF16), w_up[:, D_FF:].astype(BF16), conv_w.astype(F32),
                _row(conv_b), w_down.astype(BF16), _row(ln_g), _row(ln_b), seq=seq)


def kernel(x, positions, l0_w_in, rwkv_mix, rwkv_w0, rwkv_w2, rwkv_a0, rwkv_a2, rwkv_g2, rwkv_k_k, rwkv_k_a, rwkv_r_k, rwkv_ln_g, rwkv_ln_b, ssm_conv_w, ssm_conv_b, ssm_dt_bias, ssm_a_log, ssm_d, ssm_norm_g, l0_w_out, l0_ln1_g, l0_ln1_b, ffn0_w_up, ffn0_conv_w, ffn0_conv_b, ffn0_w_down, l0_ln2_g, l0_ln2_b, l1_w_in, mla_q_norm_g, mla_w_uq, mla_kv_norm_g, mla_w_ukv, l1_w_out, l1_ln1_g, l1_ln1_b, ffn1_w_up, ffn1_conv_w, ffn1_conv_b, ffn1_w_down, l1_ln2_g, l1_ln2_b):
    batch, seq, _ = x.shape
    assert batch * N_HEADS == LANES, "the RWKV scan maps batch*heads onto the lane axis"
    x2 = x.reshape(batch * seq, D_MODEL).astype(F32)

    y_a, gate, y_b = _layer0_mixer(x2, batch, seq, l0_w_in, rwkv_mix, rwkv_w0, rwkv_w2, rwkv_a0, rwkv_a2,
                                   rwkv_g2, rwkv_k_k, rwkv_k_a, rwkv_r_k, rwkv_ln_g, rwkv_ln_b, ssm_conv_w,
                                   ssm_conv_b, ssm_dt_bias, ssm_a_log, ssm_d, ssm_norm_g)
    h = _mix_out(y_a, gate, y_b, x2, l0_w_out.astype(BF16), _row(l0_ln1_g), _row(l0_ln1_b), seq=seq)
    h = _ffn_layer(h, seq, ffn0_w_up, ffn0_conv_w, ffn0_conv_b, ffn0_w_down, l0_ln2_g, l0_ln2_b)

    y_c, y_d = _layer1_mixer(h, positions, batch, seq, l1_w_in, mla_q_norm_g, mla_w_uq, mla_kv_norm_g, mla_w_ukv)
    h = _mix_out(y_c, None, y_d, h, l1_w_out.astype(BF16), _row(l1_ln1_g), _row(l1_ln1_b), seq=seq)
    h = _ffn_layer(h, seq, ffn1_w_up, ffn1_conv_w, ffn1_conv_b, ffn1_w_down, l1_ln2_g, l1_ln2_b)
    return h.reshape(batch, seq, D_MODEL).astype(x.dtype)
```

```python
import functools
import math

import jax
import jax.numpy as jnp
from jax import lax
from jax.experimental import pallas as pl
from jax.experimental.pallas import tpu as pltpu

F32 = jnp.float32
BF16 = jnp.bfloat16

D_MODEL = 1024
HEAD_DIM = 64
N_HEADS = 8
HALF = N_HEADS * HEAD_DIM
RWKV_DECAY_LORA = 64
RWKV_A_LORA = 64
RWKV_GATE_LORA = 128
RWKV_GN_EPS = 64e-5
RWKV_COLS = 3 * HALF + RWKV_DECAY_LORA + RWKV_A_LORA + RWKV_GATE_LORA
SSM_GROUPS = 2
SSM_STATE = 128
SSM_CONV = 4
SSM_CHUNK = 128
SSM_XBC = HALF + 2 * SSM_GROUPS * SSM_STATE
SSM_COLS = HALF + SSM_XBC + N_HEADS
SSM_COLS_PAD = SSM_XBC + HALF + 128
MLA_NOPE = 64
MLA_ROPE = 32
MLA_V = 64
MLA_Q_LORA = 256
MLA_KV_LORA = 128
ROPE_THETA = 10000.0
D_FF = 2816
FFN_CONV = 3
DEPTH = 2
ALPHA = (2 * DEPTH) ** 0.25
LN_EPS = 1e-5

LANES = 128
SUBLANES = 8
VMEM_LIMIT = 56 * 1024 * 1024
NEG_BIG = -1e30

_HI = lax.Precision.HIGHEST


def _params(*sem):
    return pltpu.CompilerParams(dimension_semantics=sem, vmem_limit_bytes=VMEM_LIMIT)


def _dot(a, b, precision=None):
    return jnp.dot(a, b, preferred_element_type=F32, precision=precision)


def _dot_nt(a, b):
    return lax.dot_general(a, b, (((1,), (1,)), ((), ())), preferred_element_type=F32)


def _sigmoid(x):
    return 1.0 / (1.0 + jnp.exp(-x))


def _softplus(x):
    return jnp.maximum(x, 0.0) + jnp.log1p(jnp.exp(-jnp.abs(x)))


def _silu(x):
    return x * _sigmoid(x)


def _layer_norm(v, g, b):
    mu = jnp.mean(v, axis=-1, keepdims=True)
    d = v - mu
    var = jnp.mean(d * d, axis=-1, keepdims=True)
    return d * lax.rsqrt(var + LN_EPS) * g + b


def _const_spec(shape):
    nd = len(shape)
    return pl.BlockSpec(shape, lambda *_: (0,) * nd)


def _rwkv_in_body(x_ref, w_ref, mix_ref, wl_ref, w0a0_ref, g2_ref, r_ref, k_ref, v_ref, dec_ref, a_ref, g_ref,
                  carry_ref, *, tiles_per_seq):
    i = pl.program_id(0)
    p = _dot(x_ref[...].astype(BF16), w_ref[...])
    tm = p.shape[0]
    first = (i % tiles_per_seq) == 0
    prev_row = jnp.where(first, 0.0, carry_ref[SUBLANES - 1:SUBLANES, :])
    row = lax.broadcasted_iota(jnp.int32, p.shape, 0)
    prev = jnp.where(row == 0, prev_row, pltpu.roll(p, 1, 0))
    carry_ref[...] = p[tm - SUBLANES:tm, :]
    pm = p + (prev - p) * mix_ref[...]
    lo = pm[:, 3 * HALF:3 * HALF + LANES]
    lane = lax.broadcasted_iota(jnp.int32, lo.shape, 1)
    lo = jnp.where(lane < RWKV_DECAY_LORA, jnp.tanh(lo), lo)
    wa = _dot(lo.astype(BF16), wl_ref[...]) + w0a0_ref[...]
    log_w = -_softplus(-wa[:, :HALF]) - 0.5
    r_ref[...] = pm[:, 0:HALF]
    k_ref[...] = pm[:, HALF:2 * HALF]
    v_ref[...] = pm[:, 2 * HALF:3 * HALF]
    dec_ref[...] = jnp.exp(-jnp.exp(log_w))
    a_ref[...] = _sigmoid(wa[:, HALF:])
    g_lo = _sigmoid(pm[:, 3 * HALF + LANES:])
    g_ref[...] = _dot(g_lo.astype(BF16), g2_ref[...])


def _rwkv_in(x2, w_r, mix, wl, w0a0, g2, *, seq, tm=256):
    m = x2.shape[0]
    tps = seq // tm
    batch = m // seq
    body = functools.partial(_rwkv_in_body, tiles_per_seq=tps)
    time_major = pl.BlockSpec((tm, HALF), lambda i: (i % tps, i // tps))
    return pl.pallas_call(
        body,
        grid=(m // tm,),
        in_specs=[
            pl.BlockSpec((tm, D_MODEL), lambda i: (i, 0)),
            _const_spec(w_r.shape), _const_spec(mix.shape), _const_spec(wl.shape),
            _const_spec(w0a0.shape), _const_spec(g2.shape),
        ],
        out_specs=[time_major] * 5 + [pl.BlockSpec((tm, HALF), lambda i: (i, 0))],
        out_shape=[jax.ShapeDtypeStruct((seq, batch * HALF), F32)] * 5 + [jax.ShapeDtypeStruct((m, HALF), F32)],
        scratch_shapes=[pltpu.VMEM((SUBLANES, RWKV_COLS), F32)],
        compiler_params=_params("arbitrary"),
        name="rwkv_in",
    )(x2, w_r, mix, wl, w0a0, g2)


def _proj_body(x_ref, w_ref, o_ref):
    o_ref[...] = _dot(x_ref[...].astype(BF16), w_ref[...])


def _proj(x2, w, *, tm=512):
    m, k = x2.shape
    n = w.shape[1]
    return pl.pallas_call(
        _proj_body,
        grid=(m // tm,),
        in_specs=[pl.BlockSpec((tm, k), lambda i: (i, 0)), _const_spec(w.shape)],
        out_specs=pl.BlockSpec((tm, n), lambda i: (i, 0)),
        out_shape=jax.ShapeDtypeStruct((m, n), F32),
        compiler_params=_params("arbitrary"),
        name="proj",
    )(x2, w)


SCAN_UNROLL = 16


def _to_chain_lanes(z):
    zt = z.T
    return jnp.concatenate([zt[:HEAD_DIM], zt[HEAD_DIM:]], axis=1)


def _from_chain_lanes(y):
    return jnp.concatenate([y[:, :LANES // 2], y[:, LANES // 2:]], axis=0).T


def _rwkv_scan_body(r_ref, k_ref, v_ref, w_ref, a_ref, kk_ref, ka_ref, rk_ref, lng_ref, lnb_ref, o_ref,
                    h_ref, rs_ref, ws_ref, vs_ref, av_ref, bv_ref, k2_ref, bon_ref, ys_ref, *, tb):
    @pl.when(pl.program_id(0) == 0)
    def _():
        h_ref[...] = jnp.zeros_like(h_ref)

    for t in range(tb):
        r = _to_chain_lanes(r_ref[t])
        k = _to_chain_lanes(k_ref[t])
        v = _to_chain_lanes(v_ref[t])
        a = _to_chain_lanes(a_ref[t])
        kk = k * kk_ref[...]
        nrm = jnp.sqrt(jnp.sum(kk * kk, axis=0, keepdims=True))
        kk = kk / jnp.maximum(nrm, 1e-12)
        k2 = k * (1.0 + (a - 1.0) * ka_ref[...])
        rs_ref[t] = r
        vs_ref[t] = v
        ws_ref[t] = _to_chain_lanes(w_ref[t])
        av_ref[t] = -kk
        bv_ref[t] = kk * a
        k2_ref[t] = k2
        bon_ref[t] = jnp.sum(r * k2 * rk_ref[...], axis=0, keepdims=True) * v

    def step(t, carry):
        zero = jnp.zeros((HEAD_DIM, LANES), F32)

        def read(j, u):
            return u + h_ref[j] * av_ref[t, pl.ds(j, 1), :]

        u = lax.fori_loop(0, HEAD_DIM, read, zero, unroll=SCAN_UNROLL)
        vt = vs_ref[t]

        def update(j, y):
            wj = ws_ref[t, pl.ds(j, 1), :]
            rj = rs_ref[t, pl.ds(j, 1), :]
            bj = bv_ref[t, pl.ds(j, 1), :]
            kj = k2_ref[t, pl.ds(j, 1), :]
            hn = h_ref[j] * wj + u * bj + vt * kj
            h_ref[j] = hn
            return y + hn * rj

        ys_ref[t] = lax.fori_loop(0, HEAD_DIM, update, zero, unroll=SCAN_UNROLL)
        return carry

    lax.fori_loop(0, tb, step, 0)

    for t in range(tb):
        y = ys_ref[t]
        mu = jnp.mean(y, axis=0, keepdims=True)
        d = y - mu
        var = jnp.mean(d * d, axis=0, keepdims=True)
        y = d * lax.rsqrt(var + RWKV_GN_EPS) * lng_ref[...] + lnb_ref[...] + bon_ref[t]
        o_ref[t] = _from_chain_lanes(y)


def _rwkv_scan(r, k, v, w, a, kk, ka, rk, lng, lnb, *, tb=16):
    t = r.shape[0]
    vec = pl.BlockSpec((HEAD_DIM, LANES), lambda i: (0, 0))
    blk = (tb, HEAD_DIM, LANES)
    seq_spec = pl.BlockSpec(blk, lambda i: (i, 0, 0))
    return pl.pallas_call(
        functools.partial(_rwkv_scan_body, tb=tb),
        grid=(t // tb,),
        in_specs=[seq_spec] * 5 + [vec] * 5,
        out_specs=seq_spec,
        out_shape=jax.ShapeDtypeStruct((t, HEAD_DIM, LANES), F32),
        scratch_shapes=[pltpu.VMEM((HEAD_DIM, HEAD_DIM, LANES), F32)] + [pltpu.VMEM(blk, F32)] * 8,
        compiler_params=_params("arbitrary"),
        name="rwkv_scan",
    )(r, k, v, w, a, kk, ka, rk, lng, lnb)


def _ssd_body(xbc_ref, z_ref, dt_ref, dtt_ref, cw_ref, cb_ref, dtb_row_ref, dtb_col_ref,
              a_row_ref, a_col_ref, e_ref, dsk_ref, ng_ref, o_ref, ext_ref, st_ref):
    L = SSM_CHUNK

    @pl.when(pl.program_id(1) == 0)
    def _():
        ext_ref[...] = jnp.zeros_like(ext_ref)
        st_ref[...] = jnp.zeros_like(st_ref)

    xbc = xbc_ref[...]
    tail = ext_ref[...]
    ext_ref[...] = xbc[L - SUBLANES:L, :]
    conv = cb_ref[...] + cw_ref[SSM_CONV - 1:SSM_CONV, :] * xbc
    for s in range(1, SSM_CONV):
        conv = conv + cw_ref[SSM_CONV - 1 - s:SSM_CONV - s, :] * _shift_rows(xbc, s, tail)
    xc = _silu(conv)
    xs = xc[:, :HALF]
    bm = xc[:, HALF:HALF + SSM_GROUPS * SSM_STATE]
    cm = xc[:, HALF + SSM_GROUPS * SSM_STATE:]

    row = lax.broadcasted_iota(jnp.int32, (L, L), 0)
    col = lax.broadcasted_iota(jnp.int32, (L, L), 1)
    lower = row >= col
    dtc = _softplus(dt_ref[...] + dtb_row_ref[...])
    cum_col = _dot(lower.astype(F32), dtc * a_row_ref[...], _HI)
    dtr = _softplus(dtt_ref[0] + dtb_col_ref[...])
    cum_row = _dot(dtr * a_col_ref[...], (row <= col).astype(F32), _HI)
    dtx = _dot(dtc, e_ref[...], _HI)
    ccx = _dot(cum_col, e_ref[...], _HI)
    x = xs * dtx
    ecc = jnp.exp(ccx)
    clast = ccx[L - 1:L, :]
    xd = x * jnp.exp(clast - ccx)
    elast = jnp.exp(clast)
    lane = lax.broadcasted_iota(jnp.int32, (L, LANES), 1)

    ys = []
    for g in range(SSM_GROUPS):
        bg = bm[:, g * SSM_STATE:(g + 1) * SSM_STATE]
        cg = cm[:, g * SSM_STATE:(g + 1) * SSM_STATE].astype(BF16)
        cbm = _dot_nt(cg, bg.astype(BF16))
        bgt = bg.T.astype(BF16)
        for q in range(2):
            p = g * 2 + q
            sl = slice(p * LANES, (p + 1) * LANES)
            st = st_ref[p]
            yp = _dot(cg, st.astype(BF16)) * ecc[:, sl]
            xp = x[:, sl]
            for hh in range(2):
                h = 2 * p + hh
                seg = cum_col[:, h:h + 1] - cum_row[h:h + 1, :]
                dec = jnp.exp(jnp.where(lower, seg, NEG_BIG))
                xh = jnp.where((lane // HEAD_DIM) == hh, xp, 0.0).astype(BF16)
                yp = yp + _dot((cbm * dec).astype(BF16), xh)
            st_ref[p] = st * elast[:, sl] + _dot(bgt, xd[:, sl].astype(BF16))
            ys.append(yp)
    y = jnp.concatenate(ys, axis=1) + xs * dsk_ref[...]
    u = y * _silu(z_ref[...])
    gw = HALF // SSM_GROUPS
    outs = []
    for g in range(SSM_GROUPS):
        ug = u[:, g * gw:(g + 1) * gw]
        outs.append(ug * lax.rsqrt(jnp.mean(ug * ug, axis=-1, keepdims=True) + 1e-5))
    o_ref[...] = (jnp.concatenate(outs, axis=1) * ng_ref[...]).astype(o_ref.dtype)


def _ssd(ps, dtt, cw, cb, dtb_row, dtb_col, a_row, a_col, e, dsk, ng, *, batch, seq):
    L = SSM_CHUNK
    nc = seq // L
    consts = [cw, cb, dtb_row, dtb_col, a_row, a_col, e, dsk, ng]
    return pl.pallas_call(
        _ssd_body,
        grid=(batch, nc),
        in_specs=[
            pl.BlockSpec((L, SSM_XBC), lambda b, c: (b * nc + c, 0)),
            pl.BlockSpec((L, HALF), lambda b, c: (b * nc + c, SSM_XBC // HALF)),
            pl.BlockSpec((L, LANES), lambda b, c: (b * nc + c, (SSM_XBC + HALF) // LANES)),
            pl.BlockSpec((1, N_HEADS, L), lambda b, c: (b, 0, c)),
        ] + [_const_spec(c.shape) for c in consts],
        out_specs=pl.BlockSpec((L, HALF), lambda b, c: (b * nc + c, 0)),
        out_shape=jax.ShapeDtypeStruct((batch * seq, HALF), BF16),
        scratch_shapes=[pltpu.VMEM((SUBLANES, SSM_XBC), F32),
                        pltpu.VMEM((N_HEADS // 2, SSM_STATE, LANES), F32)],
        compiler_params=_params("arbitrary", "arbitrary"),
        name="ssd",
    )(ps, ps, ps, dtt, *consts)


def _mix_out_body(*refs, gated):
    if gated:
        ya_ref, g_ref, yb_ref, x_ref, w_ref, lng_ref, lnb_ref, o_ref = refs
        ya = ya_ref[...] * g_ref[...]
    else:
        ya_ref, yb_ref, x_ref, w_ref, lng_ref, lnb_ref, o_ref = refs
        ya = ya_ref[...]
    mixed = _dot(ya.astype(BF16), w_ref[0:HALF, :]) + _dot(yb_ref[...].astype(BF16), w_ref[HALF:, :])
    o_ref[...] = _layer_norm(ALPHA * x_ref[...] + mixed, lng_ref[...], lnb_ref[...])


def _mix_out(ya, g, yb, x2, w, lng, lnb, *, seq, tm=512):
    m = x2.shape[0]
    tm = min(tm, seq)
    tps = seq // tm
    half = pl.BlockSpec((tm, HALF), lambda i: (i, 0))
    full = pl.BlockSpec((tm, D_MODEL), lambda i: (i, 0))
    gated = g is not None
    acts = [ya, g, yb] if gated else [ya, yb]
    act_specs = [pl.BlockSpec((tm, HALF), lambda i: (i % tps, i // tps)), half, half] if gated else [half, half]
    return pl.pallas_call(
        functools.partial(_mix_out_body, gated=gated),
        grid=(m // tm,),
        in_specs=act_specs + [full, _const_spec(w.shape), _const_spec(lng.shape), _const_spec(lnb.shape)],
        out_specs=full,
        out_shape=jax.ShapeDtypeStruct((m, D_MODEL), F32),
        compiler_params=_params("arbitrary"),
        name="mix_out",
    )(*acts, x2, w, lng, lnb)


def _shift_rows(g, s, tail):
    rolled = pltpu.roll(g, s, 0)
    head = rolled[:SUBLANES]
    row = lax.broadcasted_iota(jnp.int32, head.shape, 0)
    for r in range(s):
        head = jnp.where(row == r, tail[SUBLANES - s + r:SUBLANES - s + r + 1, :], head)
    return jnp.concatenate([head, rolled[SUBLANES:]], axis=0)


def _ffn_body(x_ref, wg_ref, wu_ref, cw_ref, cb_ref, wd_ref, lng_ref, lnb_ref, o_ref,
              xb_ref, acc_ref, act_ref, carry_ref, *, tiles_per_seq, chunk):
    i = pl.program_id(0)
    f = pl.program_id(1)
    nf = pl.num_programs(1)
    tm, tf = act_ref.shape

    @pl.when(f == 0)
    def _():
        xb_ref[...] = x_ref[...].astype(BF16)

    xb = xb_ref[...]
    first = (i % tiles_per_seq) == 0
    for c0 in range(0, tf, chunk):
        cs = slice(c0, min(c0 + chunk, tf))
        gate = _dot(xb, wg_ref[:, cs])
        up = _dot(xb, wu_ref[:, cs])
        tail = jnp.where(first, 0.0, carry_ref[f, :, cs])
        carry_ref[f, :, cs] = gate[tm - SUBLANES:tm, :]
        conv = cb_ref[:, cs] + cw_ref[FFN_CONV - 1:FFN_CONV, cs] * gate
        for s in range(1, FFN_CONV):
            conv = conv + cw_ref[FFN_CONV - 1 - s:FFN_CONV - s, cs] * _shift_rows(gate, s, tail)
        act_ref[:, cs] = (_silu(conv) * up).astype(BF16)
    down = _dot(act_ref[...], wd_ref[...])

    @pl.when(f == 0)
    def _():
        acc_ref[...] = down

    @pl.when(f == nf - 1)
    def _():
        o_ref[...] = _layer_norm(ALPHA * x_ref[...] + acc_ref[...] + down, lng_ref[...], lnb_ref[...])


def _ffn(x2, wg, wu, cw, cb, wd, lng, lnb, *, seq, tm=512, tf=D_FF // 2, chunk=256):
    m = x2.shape[0]
    tm = min(tm, seq)
    nf = D_FF // tf
    assert nf == 2, "the accumulator is written on the first d_ff tile and consumed on the last"
    return pl.pallas_call(
        functools.partial(_ffn_body, tiles_per_seq=seq // tm, chunk=chunk),
        grid=(m // tm, nf),
        in_specs=[
            pl.BlockSpec((tm, D_MODEL), lambda i, f: (i, 0)),
            pl.BlockSpec((D_MODEL, tf), lambda i, f: (0, f)),
            pl.BlockSpec((D_MODEL, tf), lambda i, f: (0, f)),
            pl.BlockSpec((FFN_CONV, tf), lambda i, f: (0, f)),
            pl.BlockSpec((1, tf), lambda i, f: (0, f)),
            pl.BlockSpec((tf, D_MODEL), lambda i, f: (f, 0)),
            _const_spec(lng.shape), _const_spec(lnb.shape),
        ],
        out_specs=pl.BlockSpec((tm, D_MODEL), lambda i, f: (i, 0)),
        out_shape=jax.ShapeDtypeStruct((m, D_MODEL), F32),
        scratch_shapes=[pltpu.VMEM((tm, D_MODEL), BF16), pltpu.VMEM((tm, D_MODEL), F32),
                        pltpu.VMEM((tm, tf), BF16), pltpu.VMEM((nf, SUBLANES, tf), F32)],
        compiler_params=_params("arbitrary", "arbitrary"),
        name="conv_ffn",
    )(x2, wg, wu, cw, cb, wd, lng, lnb)


SB_SCALE = HEAD_DIM ** -0.5
assert math.frexp(SB_SCALE)[0] == 0.5, "folded into q before the bf16 cast, so it must be a power of two"
PAIR = 2 * LANES
N_PAIRS = N_HEADS // 2
L1_SB = 3 * HALF
L1_COLS_PAD = L1_SB + MLA_Q_LORA + MLA_KV_LORA + 2 * LANES


def _store_head_values(v_ref, v, even_head):
    v_ref[:, :HALF] = jnp.where(even_head, v, 0.0).astype(BF16)
    v_ref[:, HALF:] = jnp.where(even_head, 0.0, v).astype(BF16)


def _l1_in_body(x_ref, pos_ref, w_ref, invf_ref, qg_ref, wuq_ref, kvg_ref, wukv_ref,
                qsb_ref, ksb_ref, vsb_ref, q_ref, k_ref, v_ref):
    p = _dot(x_ref[...].astype(BF16), w_ref[...])
    qsb_ref[...] = (p[:, 0:HALF] * SB_SCALE).astype(BF16)
    ksb_ref[...] = p[:, HALF:2 * HALF].astype(BF16)
    lane = lax.broadcasted_iota(jnp.int32, (p.shape[0], HALF), 1)
    even_head = ((lane // HEAD_DIM) & 1) == 0
    _store_head_values(vsb_ref, p[:, 2 * HALF:3 * HALF], even_head)
    c_q = p[:, L1_SB:L1_SB + MLA_Q_LORA]
    c_kv = p[:, L1_SB + MLA_Q_LORA:L1_SB + MLA_Q_LORA + MLA_KV_LORA]
    off = L1_SB + MLA_Q_LORA + MLA_KV_LORA
    kpe = p[:, off:off + LANES]
    kpe_rot = p[:, off + LANES:off + 2 * LANES]
    ang = pos_ref[...].astype(F32) * invf_ref[...]
    cos = jnp.cos(ang)
    sin = jnp.sin(ang)
    kpe = (kpe * cos + kpe_rot * sin).astype(BF16)
    cqn = c_q * lax.rsqrt(jnp.mean(c_q * c_q, axis=-1, keepdims=True) + 1e-6) * qg_ref[...]
    q = _dot(cqn.astype(BF16), wuq_ref[...])
    ckn = c_kv * lax.rsqrt(jnp.mean(c_kv * c_kv, axis=-1, keepdims=True) + 1e-6) * kvg_ref[...]
    kv = _dot(ckn.astype(BF16), wukv_ref[...])
    for pr in range(N_PAIRS):
        q_ref[:, pr * PAIR:pr * PAIR + LANES] = q[:, pr * PAIR:pr * PAIR + LANES].astype(BF16)
        q_pe = q[:, pr * PAIR + LANES:(pr + 1) * PAIR]
        q_rot = q[:, N_PAIRS * PAIR + pr * LANES:N_PAIRS * PAIR + (pr + 1) * LANES]
        q_ref[:, pr * PAIR + LANES:(pr + 1) * PAIR] = (q_pe * cos + q_rot * sin).astype(BF16)
        k_ref[:, pr * PAIR:pr * PAIR + LANES] = kv[:, pr * LANES:(pr + 1) * LANES].astype(BF16)
        k_ref[:, pr * PAIR + LANES:(pr + 1) * PAIR] = kpe
    _store_head_values(v_ref, kv[:, HALF:], even_head)


def _l1_in(x2, pos2, w, invf, qg, wuq, kvg, wukv, *, tm=512):
    m = x2.shape[0]
    consts = [w, invf, qg, wuq, kvg, wukv]
    half = pl.BlockSpec((tm, HALF), lambda i: (i, 0))
    cat = pl.BlockSpec((tm, N_PAIRS * PAIR), lambda i: (i, 0))
    assert N_PAIRS * PAIR == 2 * HALF
    sd = jax.ShapeDtypeStruct
    return pl.pallas_call(
        _l1_in_body,
        grid=(m // tm,),
        in_specs=[pl.BlockSpec((tm, D_MODEL), lambda i: (i, 0)), pl.BlockSpec((tm, 1), lambda i: (i, 0))]
        + [_const_spec(c.shape) for c in consts],
        out_specs=[half, half, cat, cat, cat, cat],
        out_shape=[sd((m, HALF), BF16)] * 2 + [sd((m, 2 * HALF), BF16)] * 4,
        compiler_params=_params("arbitrary"),
        name="l1_in",
    )(x2, pos2, *consts)


ATT_BLOCK = 256


def _stack_heads(q_ref, qs_ref, masks, width):
    tq = ATT_BLOCK
    for p in range(N_PAIRS):
        qp = q_ref[:, p * width:(p + 1) * width]
        for hh in range(2):
            qs_ref[p, hh * tq:(hh + 1) * tq, :] = jnp.where(masks[hh], qp, jnp.zeros_like(qp))


def _head_values(v_ref, start, p):
    return [v_ref[pl.ds(start, ATT_BLOCK), hh * HALF + p * LANES:hh * HALF + (p + 1) * LANES] for hh in range(2)]


def _sb_body(q_ref, k_ref, v_ref, o_ref, qs_ref, acc_ref, s_ref, w_ref):
    qi = pl.program_id(1)
    tq = ATT_BLOCK
    lane = lax.broadcasted_iota(jnp.int32, (tq, LANES), 1)
    _stack_heads(q_ref, qs_ref, [(lane // HEAD_DIM) == hh for hh in range(2)], LANES)
    acc_ref[...] = jnp.zeros_like(acc_ref)
    row = lax.broadcasted_iota(jnp.int32, (2 * tq, tq), 0) & (tq - 1)
    col = lax.broadcasted_iota(jnp.int32, (2 * tq, tq), 1)
    strict = col < row
    later = ((lax.broadcasted_iota(jnp.int32, (2 * tq, tq), 0) & (tq - 1))
             > lax.broadcasted_iota(jnp.int32, (2 * tq, tq), 1)).astype(BF16)

    def scores(kj, slot):
        start = pl.multiple_of(kj * tq, tq)
        for p in range(N_PAIRS):
            s_ref[slot, p] = _dot_nt(qs_ref[p], k_ref[pl.ds(start, tq), p * LANES:(p + 1) * LANES])

    def weights(slot, runs, diagonal):
        new_runs = []
        for p in range(N_PAIRS):
            z = s_ref[slot, p]
            log_beta = jnp.minimum(z, 0.0) - jnp.log(1.0 + jnp.exp(-jnp.abs(z)))
            log_keep = log_beta - z
            if diagonal:
                log_keep = jnp.where(strict, log_keep, 0.0)
            hi = log_keep.astype(BF16)
            lo = (log_keep - hi.astype(F32)).astype(BF16)
            suffix = _dot(jnp.concatenate([hi, lo], axis=1), later)
            att = jnp.exp(log_beta + suffix + runs[p])
            if diagonal:
                att = jnp.where(strict, att, 0.0)
            w_ref[slot, p] = att.astype(BF16)
            new_runs.append(runs[p] + jnp.sum(log_keep, axis=1, keepdims=True))
        return tuple(new_runs)

    def accumulate(kj, slot):
        start = pl.multiple_of(kj * tq, tq)
        for p in range(N_PAIRS):
            vh = _head_values(v_ref, start, p)
            w = w_ref[slot, p]
            acc_ref[p] += jnp.concatenate([_dot(w[:tq], vh[0]), _dot(w[tq:], vh[1])], axis=0)

    scores(qi, 0)
    scores(jnp.maximum(qi - 1, 0), 1)
    runs = weights(0, tuple(jnp.zeros((2 * tq, 1), F32) for _ in range(N_PAIRS)), True)

    def trip(i, runs):
        slot = i & 1
        accumulate(qi - i + 1, 1 - slot)
        runs = weights(slot, runs, False)
        scores(jnp.maximum(qi - i - 1, 0), 1 - slot)
        return runs

    lax.fori_loop(1, qi + 1, trip, runs)
    accumulate(0, qi & 1)
    for p in range(N_PAIRS):
        o_ref[:, p * LANES:(p + 1) * LANES] = (acc_ref[p, :tq] + acc_ref[p, tq:]).astype(o_ref.dtype)


def _sb_attention(q, k, v, *, batch, seq):
    tq = ATT_BLOCK
    nq = seq // tq
    return pl.pallas_call(
        _sb_body,
        grid=(batch, nq),
        in_specs=[
            pl.BlockSpec((tq, HALF), lambda b, i: (b * nq + i, 0)),
            pl.BlockSpec((seq, HALF), lambda b, i: (b, 0)),
            pl.BlockSpec((seq, 2 * HALF), lambda b, i: (b, 0)),
        ],
        out_specs=pl.BlockSpec((tq, HALF), lambda b, i: (b * nq + i, 0)),
        out_shape=jax.ShapeDtypeStruct((batch * seq, HALF), BF16),
        scratch_shapes=[pltpu.VMEM((N_PAIRS, 2 * tq, LANES), BF16), pltpu.VMEM((N_PAIRS, 2 * tq, LANES), F32),
                        pltpu.VMEM((2, N_PAIRS, 2 * tq, tq), F32), pltpu.VMEM((2, N_PAIRS, 2 * tq, tq), BF16)],
        compiler_params=_params("arbitrary", "arbitrary"),
        name="sb_attention",
    )(q, k, v)


def _mla_body(q_ref, k_ref, v_ref, o_ref, qs_ref, acc_ref):
    qi = pl.program_id(1)
    tq = ATT_BLOCK
    scale = (MLA_NOPE + MLA_ROPE) ** -0.5
    lane2 = lax.broadcasted_iota(jnp.int32, (tq, PAIR), 1)
    masks = [((lane2 >= hh * MLA_NOPE) & (lane2 < (hh + 1) * MLA_NOPE))
             | ((lane2 >= LANES + hh * MLA_ROPE) & (lane2 < LANES + (hh + 1) * MLA_ROPE)) for hh in range(2)]
    _stack_heads(q_ref, qs_ref, masks, PAIR)
    acc_ref[...] = jnp.zeros_like(acc_ref)
    row = lax.broadcasted_iota(jnp.int32, (2 * tq, tq), 0) & (tq - 1)
    col = lax.broadcasted_iota(jnp.int32, (2 * tq, tq), 1)
    causal = col <= row

    def block(kj, carry, diagonal):
        start = pl.multiple_of(kj * tq, tq)
        new = []
        for p in range(N_PAIRS):
            m, l = carry[p]
            k = k_ref[pl.ds(start, tq), p * PAIR:(p + 1) * PAIR]
            vh = _head_values(v_ref, start, p)
            s = _dot_nt(qs_ref[p], k) * scale
            if diagonal:
                s = jnp.where(causal, s, NEG_BIG)
            m_new = jnp.maximum(m, jnp.max(s, axis=1, keepdims=True))
            corr = jnp.exp(m - m_new)
            pexp = jnp.exp(s - m_new)
            l = l * corr + jnp.sum(pexp, axis=1, keepdims=True)
            pexp = pexp.astype(BF16)
            pv = jnp.concatenate([_dot(pexp[:tq], vh[0]), _dot(pexp[tq:], vh[1])], axis=0)
            acc_ref[p] = acc_ref[p] * corr + pv
            new.append((m_new, l))
        return tuple(new)

    init = tuple((jnp.full((2 * tq, 1), NEG_BIG, F32), jnp.zeros((2 * tq, 1), F32)) for _ in range(N_PAIRS))
    carry = block(qi, init, True)
    carry = lax.fori_loop(0, qi, lambda j, c: block(j, c, False), carry)
    for p in range(N_PAIRS):
        out = acc_ref[p] / carry[p][1]
        o_ref[:, p * LANES:(p + 1) * LANES] = (out[:tq] + out[tq:]).astype(o_ref.dtype)


def _mla_attention(q, k, v, *, batch, seq):
    tq = ATT_BLOCK
    nq = seq // tq
    return pl.pallas_call(
        _mla_body,
        grid=(batch, nq),
        in_specs=[
            pl.BlockSpec((tq, N_PAIRS * PAIR), lambda b, i: (b * nq + i, 0)),
            pl.BlockSpec((seq, N_PAIRS * PAIR), lambda b, i: (b, 0)),
            pl.BlockSpec((seq, 2 * HALF), lambda b, i: (b, 0)),
        ],
        out_specs=pl.BlockSpec((tq, HALF), lambda b, i: (b * nq + i, 0)),
        out_shape=jax.ShapeDtypeStruct((batch * seq, HALF), BF16),
        scratch_shapes=[pltpu.VMEM((N_PAIRS, 2 * tq, PAIR), BF16), pltpu.VMEM((N_PAIRS, 2 * tq, LANES), F32)],
        compiler_params=_params("arbitrary", "arbitrary"),
        name="mla_attention",
    )(q, k, v)


def _row(v):
    return v.reshape(1, -1).astype(F32)


def _chain_layout(v):
    lane = jnp.arange(LANES)
    head = (2 * (lane % (LANES // 2)) + lane // (LANES // 2)) % N_HEADS
    return v.reshape(N_HEADS, HEAD_DIM).T[:, head].astype(F32)


def _rope_rotate_cols(w):
    half = MLA_ROPE // 2
    return jnp.concatenate([-w[..., half:], w[..., :half]], axis=-1)


def _layer0_mixer(x2, batch, seq, l0_w_in, rwkv_mix, rwkv_w0, rwkv_w2, rwkv_a0, rwkv_a2, rwkv_g2,
                  rwkv_k_k, rwkv_k_a, rwkv_r_k, rwkv_ln_g, rwkv_ln_b, ssm_conv_w, ssm_conv_b,
                  ssm_dt_bias, ssm_a_log, ssm_d, ssm_norm_g):
    w_r = l0_w_in[:, :RWKV_COLS].astype(BF16)
    zeros = jnp.zeros((RWKV_DECAY_LORA, HALF), F32)
    wl = jnp.concatenate([jnp.concatenate([rwkv_w2, zeros], axis=1),
                          jnp.concatenate([zeros, rwkv_a2], axis=1)], axis=0).astype(BF16)
    w0a0 = _row(jnp.concatenate([rwkv_w0, rwkv_a0]))
    *rkvwa, gate = _rwkv_in(x2, w_r, _row(rwkv_mix), wl, w0a0, rwkv_g2.astype(BF16), seq=seq)
    as_tiles = lambda t: t.reshape(seq, HEAD_DIM, LANES)
    y = _rwkv_scan(*[as_tiles(t) for t in rkvwa], _chain_layout(rwkv_k_k), _chain_layout(rwkv_k_a),
                   _chain_layout(rwkv_r_k.reshape(-1)), _chain_layout(rwkv_ln_g), _chain_layout(rwkv_ln_b))
    y_a = y.reshape(seq, batch * HALF)

    w_s = l0_w_in[:, RWKV_COLS:]
    pad = jnp.zeros((D_MODEL, LANES - N_HEADS), F32)
    w_s = jnp.concatenate([w_s[:, HALF:HALF + SSM_XBC], w_s[:, :HALF], w_s[:, HALF + SSM_XBC:], pad], axis=1)
    ps = _proj(x2, w_s.astype(BF16))
    dt_raw = ps[:, SSM_XBC + HALF:SSM_XBC + HALF + N_HEADS]
    dtt = dt_raw.reshape(batch, seq, N_HEADS).transpose(0, 2, 1)
    a_neg = -jnp.exp(ssm_a_log.astype(F32))
    lane_pad = jnp.zeros((LANES - N_HEADS,), F32)
    expand = jnp.concatenate([jnp.repeat(jnp.eye(N_HEADS, dtype=F32), HEAD_DIM, axis=1),
                              jnp.zeros((LANES - N_HEADS, HALF), F32)], axis=0)
    y_b = _ssd(ps, dtt, ssm_conv_w.astype(F32), _row(ssm_conv_b),
               _row(jnp.concatenate([ssm_dt_bias, lane_pad])), ssm_dt_bias.reshape(N_HEADS, 1).astype(F32),
               _row(jnp.concatenate([a_neg, lane_pad])), a_neg.reshape(N_HEADS, 1),
               expand, _row(jnp.repeat(ssm_d, HEAD_DIM)), _row(ssm_norm_g), batch=batch, seq=seq)
    return y_a, gate, y_b


def _layer1_mixer(h2, positions, batch, seq, l1_w_in, mla_q_norm_g, mla_w_uq, mla_kv_norm_g, mla_w_ukv):
    off = L1_SB + MLA_Q_LORA + MLA_KV_LORA
    w_kpe = l1_w_in[:, off:off + MLA_ROPE]
    lane_zeros = jnp.zeros((D_MODEL, LANES - 2 * MLA_ROPE), F32)
    w_kpe_blk = jnp.concatenate([w_kpe, w_kpe, lane_zeros], axis=1)
    w_kpe_rot = _rope_rotate_cols(w_kpe)
    w_kpe_rot_blk = jnp.concatenate([w_kpe_rot, w_kpe_rot, lane_zeros], axis=1)
    w_in = jnp.concatenate([l1_w_in[:, :off], w_kpe_blk, w_kpe_rot_blk], axis=1).astype(BF16)

    inv_freq = 1.0 / (ROPE_THETA ** (jnp.arange(0, MLA_ROPE, 2, dtype=F32) / MLA_ROPE))
    invf = jnp.concatenate([jnp.tile(inv_freq, 4), jnp.zeros((LANES - 2 * MLA_ROPE,), F32)]).reshape(1, LANES)

    wq = mla_w_uq.reshape(MLA_Q_LORA, N_HEADS, MLA_NOPE + MLA_ROPE)
    wq_nope = wq[:, :, :MLA_NOPE].reshape(MLA_Q_LORA, N_PAIRS, 2 * MLA_NOPE)
    wq_pe = wq[:, :, MLA_NOPE:]
    lz = jnp.zeros((MLA_Q_LORA, N_PAIRS, LANES - 2 * MLA_ROPE), F32)
    wq_pe_blk = jnp.concatenate([wq_pe.reshape(MLA_Q_LORA, N_PAIRS, 2 * MLA_ROPE), lz], axis=2)
    wq_rot_blk = jnp.concatenate([_rope_rotate_cols(wq_pe).reshape(MLA_Q_LORA, N_PAIRS, 2 * MLA_ROPE), lz], axis=2)
    wuq = jnp.concatenate([jnp.concatenate([wq_nope, wq_pe_blk], axis=2).reshape(MLA_Q_LORA, N_PAIRS * PAIR),
                           wq_rot_blk.reshape(MLA_Q_LORA, N_PAIRS * LANES)], axis=1).astype(BF16)

    wkv = mla_w_ukv.reshape(MLA_KV_LORA, N_HEADS, MLA_NOPE + MLA_V)
    wukv = jnp.concatenate([wkv[:, :, :MLA_NOPE].reshape(MLA_KV_LORA, HALF),
                            wkv[:, :, MLA_NOPE:].reshape(MLA_KV_LORA, HALF)], axis=1).astype(BF16)

    qsb, ksb, vsb, q, k, v = _l1_in(h2, positions.reshape(batch * seq, 1), w_in, invf,
                                    _row(mla_q_norm_g), wuq, _row(mla_kv_norm_g), wukv)
    y_c = _sb_attention(qsb, ksb, vsb, batch=batch, seq=seq)
    y_d = _mla_attention(q, k, v, batch=batch, seq=seq)
    return y_c, y_d


def _ffn_layer(h2, seq, w_up, conv_w, conv_b, w_down, ln_g, ln_b):
    return _ffn(h2, w_up[:, :D_FF].astype(BF16), w_up[:, D_FF:].astype(BF16), conv_w.astype(F32),
                _row(conv_b), w_down.astype(BF16), _row(ln_g), _row(ln_b), seq=seq)


def kernel(x, positions, l0_w_in, rwkv_mix, rwkv_w0, rwkv_w2, rwkv_a0, rwkv_a2, rwkv_g2, rwkv_k_k, rwkv_k_a, rwkv_r_k, rwkv_ln_g, rwkv_ln_b, ssm_conv_w, ssm_conv_b, ssm_dt_bias, ssm_a_log, ssm_d, ssm_norm_g, l0_w_out, l0_ln1_g, l0_ln1_b, ffn0_w_up, ffn0_conv_w, ffn0_conv_b, ffn0_w_down, l0_ln2_g, l0_ln2_b, l1_w_in, mla_q_norm_g, mla_w_uq, mla_kv_norm_g, mla_w_ukv, l1_w_out, l1_ln1_g, l1_ln1_b, ffn1_w_up, ffn1_conv_w, ffn1_conv_b, ffn1_w_down, l1_ln2_g, l1_ln2_b):
    batch, seq, _ = x.shape
    assert batch * N_HEADS == LANES, "the RWKV scan maps batch*heads onto the lane axis"
    x2 = x.reshape(batch * seq, D_MODEL).astype(F32)

    y_a, gate, y_b = _layer0_mixer(x2, batch, seq, l0_w_in, rwkv_mix, rwkv_w0, rwkv_w2, rwkv_a0, rwkv_a2,
                                   rwkv_g2, rwkv_k_k, rwkv_k_a, rwkv_r_k, rwkv_ln_g, rwkv_ln_b, ssm_conv_w,
                                   ssm_conv_b, ssm_dt_bias, ssm_a_log, ssm_d, ssm_norm_g)
    h = _mix_out(y_a, gate, y_b, x2, l0_w_out.astype(BF16), _row(l0_ln1_g), _row(l0_ln1_b), seq=seq)
    h = _ffn_layer(h, seq, ffn0_w_up, ffn0_conv_w, ffn0_conv_b, ffn0_w_down, l0_ln2_g, l0_ln2_b)

    y_c, y_d = _layer1_mixer(h, positions, batch, seq, l1_w_in, mla_q_norm_g, mla_w_uq, mla_kv_norm_g, mla_w_ukv)
    h = _mix_out(y_c, None, y_d, h, l1_w_out.astype(BF16), _row(l1_ln1_g), _row(l1_ln1_b), seq=seq)
    h = _ffn_layer(h, seq, ffn1_w_up, ffn1_conv_w, ffn1_conv_b, ffn1_w_down, l1_ln2_g, l1_ln2_b)
    return h.reshape(batch, seq, D_MODEL).astype(x.dtype)
```

```python
import functools
import math

import jax
import jax.numpy as jnp
from jax import lax
from jax.experimental import pallas as pl
from jax.experimental.pallas import tpu as pltpu

F32 = jnp.float32
BF16 = jnp.bfloat16

D_MODEL = 1024
HEAD_DIM = 64
N_HEADS = 8
HALF = N_HEADS * HEAD_DIM
RWKV_DECAY_LORA = 64
RWKV_A_LORA = 64
RWKV_GATE_LORA = 128
RWKV_GN_EPS = 64e-5
RWKV_COLS = 3 * HALF + RWKV_DECAY_LORA + RWKV_A_LORA + RWKV_GATE_LORA
SSM_GROUPS = 2
SSM_STATE = 128
SSM_CONV = 4
SSM_CHUNK = 128
SSM_XBC = HALF + 2 * SSM_GROUPS * SSM_STATE
SSM_COLS = HALF + SSM_XBC + N_HEADS
SSM_COLS_PAD = SSM_XBC + HALF + 128
MLA_NOPE = 64
MLA_ROPE = 32
MLA_V = 64
MLA_Q_LORA = 256
MLA_KV_LORA = 128
ROPE_THETA = 10000.0
D_FF = 2816
FFN_CONV = 3
DEPTH = 2
ALPHA = (2 * DEPTH) ** 0.25
LN_EPS = 1e-5

LANES = 128
SUBLANES = 8
VMEM_LIMIT = 56 * 1024 * 1024
NEG_BIG = -1e30

_HI = lax.Precision.HIGHEST


def _params(*sem):
    return pltpu.CompilerParams(dimension_semantics=sem, vmem_limit_bytes=VMEM_LIMIT)


def _dot(a, b, precision=None):
    return jnp.dot(a, b, preferred_element_type=F32, precision=precision)


def _dot_nt(a, b):
    return lax.dot_general(a, b, (((1,), (1,)), ((), ())), preferred_element_type=F32)


def _sigmoid(x):
    return 1.0 / (1.0 + jnp.exp(-x))


def _softplus(x):
    return jnp.maximum(x, 0.0) + jnp.log1p(jnp.exp(-jnp.abs(x)))


def _silu(x):
    return x * _sigmoid(x)


def _layer_norm(v, g, b):
    mu = jnp.mean(v, axis=-1, keepdims=True)
    d = v - mu
    var = jnp.mean(d * d, axis=-1, keepdims=True)
    return d * lax.rsqrt(var + LN_EPS) * g + b


def _const_spec(shape):
    nd = len(shape)
    return pl.BlockSpec(shape, lambda *_: (0,) * nd)


def _rwkv_in_body(x_ref, w_ref, mix_ref, wl_ref, w0a0_ref, g2_ref, r_ref, k_ref, v_ref, dec_ref, a_ref, g_ref,
                  carry_ref, *, tiles_per_seq):
    i = pl.program_id(0)
    p = _dot(x_ref[...].astype(BF16), w_ref[...])
    tm = p.shape[0]
    first = (i % tiles_per_seq) == 0
    prev_row = jnp.where(first, 0.0, carry_ref[SUBLANES - 1:SUBLANES, :])
    row = lax.broadcasted_iota(jnp.int32, p.shape, 0)
    prev = jnp.where(row == 0, prev_row, pltpu.roll(p, 1, 0))
    carry_ref[...] = p[tm - SUBLANES:tm, :]
    pm = p + (prev - p) * mix_ref[...]
    lo = pm[:, 3 * HALF:3 * HALF + LANES]
    lane = lax.broadcasted_iota(jnp.int32, lo.shape, 1)
    lo = jnp.where(lane < RWKV_DECAY_LORA, jnp.tanh(lo), lo)
    wa = _dot(lo.astype(BF16), wl_ref[...]) + w0a0_ref[...]
    log_w = -_softplus(-wa[:, :HALF]) - 0.5
    r_ref[...] = pm[:, 0:HALF]
    k_ref[...] = pm[:, HALF:2 * HALF]
    v_ref[...] = pm[:, 2 * HALF:3 * HALF]
    dec_ref[...] = jnp.exp(-jnp.exp(log_w))
    a_ref[...] = _sigmoid(wa[:, HALF:])
    g_lo = _sigmoid(pm[:, 3 * HALF + LANES:])
    g_ref[...] = _dot(g_lo.astype(BF16), g2_ref[...])


def _rwkv_in(x2, w_r, mix, wl, w0a0, g2, *, seq, tm=256):
    m = x2.shape[0]
    tps = seq // tm
    batch = m // seq
    body = functools.partial(_rwkv_in_body, tiles_per_seq=tps)
    time_major = pl.BlockSpec((tm, HALF), lambda i: (i % tps, i // tps))
    return pl.pallas_call(
        body,
        grid=(m // tm,),
        in_specs=[
            pl.BlockSpec((tm, D_MODEL), lambda i: (i, 0)),
            _const_spec(w_r.shape), _const_spec(mix.shape), _const_spec(wl.shape),
            _const_spec(w0a0.shape), _const_spec(g2.shape),
        ],
        out_specs=[time_major] * 5 + [pl.BlockSpec((tm, HALF), lambda i: (i, 0))],
        out_shape=[jax.ShapeDtypeStruct((seq, batch * HALF), F32)] * 5 + [jax.ShapeDtypeStruct((m, HALF), F32)],
        scratch_shapes=[pltpu.VMEM((SUBLANES, RWKV_COLS), F32)],
        compiler_params=_params("arbitrary"),
        name="rwkv_in",
    )(x2, w_r, mix, wl, w0a0, g2)


def _proj_body(x_ref, w_ref, o_ref):
    o_ref[...] = _dot(x_ref[...].astype(BF16), w_ref[...])


def _proj(x2, w, *, tm=512):
    m, k = x2.shape
    n = w.shape[1]
    return pl.pallas_call(
        _proj_body,
        grid=(m // tm,),
        in_specs=[pl.BlockSpec((tm, k), lambda i: (i, 0)), _const_spec(w.shape)],
        out_specs=pl.BlockSpec((tm, n), lambda i: (i, 0)),
        out_shape=jax.ShapeDtypeStruct((m, n), F32),
        compiler_params=_params("arbitrary"),
        name="proj",
    )(x2, w)


SCAN_UNROLL = 16


def _to_chain_lanes(z):
    zt = z.T
    return jnp.concatenate([zt[:HEAD_DIM], zt[HEAD_DIM:]], axis=1)


def _from_chain_lanes(y):
    return jnp.concatenate([y[:, :LANES // 2], y[:, LANES // 2:]], axis=0).T


def _rwkv_scan_body(r_ref, k_ref, v_ref, w_ref, a_ref, kk_ref, ka_ref, rk_ref, lng_ref, lnb_ref, o_ref,
                    h_ref, rs_ref, ws_ref, vs_ref, av_ref, bv_ref, k2_ref, bon_ref, ys_ref, *, tb):
    @pl.when(pl.program_id(0) == 0)
    def _():
        h_ref[...] = jnp.zeros_like(h_ref)

    for t in range(tb):
        r = _to_chain_lanes(r_ref[t])
        k = _to_chain_lanes(k_ref[t])
        v = _to_chain_lanes(v_ref[t])
        a = _to_chain_lanes(a_ref[t])
        kk = k * kk_ref[...]
        nrm = jnp.sqrt(jnp.sum(kk * kk, axis=0, keepdims=True))
        kk = kk / jnp.maximum(nrm, 1e-12)
        k2 = k * (1.0 + (a - 1.0) * ka_ref[...])
        rs_ref[t] = r
        vs_ref[t] = v
        ws_ref[t] = _to_chain_lanes(w_ref[t])
        av_ref[t] = -kk
        bv_ref[t] = kk * a
        k2_ref[t] = k2
        bon_ref[t] = jnp.sum(r * k2 * rk_ref[...], axis=0, keepdims=True) * v

    def step(t, carry):
        zero = jnp.zeros((HEAD_DIM, LANES), F32)

        def read(j, u):
            return u + h_ref[j] * av_ref[t, pl.ds(j, 1), :]

        u = lax.fori_loop(0, HEAD_DIM, read, zero, unroll=SCAN_UNROLL)
        vt = vs_ref[t]

        def update(j, y):
            wj = ws_ref[t, pl.ds(j, 1), :]
            rj = rs_ref[t, pl.ds(j, 1), :]
            bj = bv_ref[t, pl.ds(j, 1), :]
            kj = k2_ref[t, pl.ds(j, 1), :]
            hn = h_ref[j] * wj + u * bj + vt * kj
            h_ref[j] = hn
            return y + hn * rj

        ys_ref[t] = lax.fori_loop(0, HEAD_DIM, update, zero, unroll=SCAN_UNROLL)
        return carry

    lax.fori_loop(0, tb, step, 0)

    for t in range(tb):
        y = ys_ref[t]
        mu = jnp.mean(y, axis=0, keepdims=True)
        d = y - mu
        var = jnp.mean(d * d, axis=0, keepdims=True)
        y = d * lax.rsqrt(var + RWKV_GN_EPS) * lng_ref[...] + lnb_ref[...] + bon_ref[t]
        o_ref[t] = _from_chain_lanes(y)


def _rwkv_scan(r, k, v, w, a, kk, ka, rk, lng, lnb, *, tb=16):
    t = r.shape[0]
    vec = pl.BlockSpec((HEAD_DIM, LANES), lambda i: (0, 0))
    blk = (tb, HEAD_DIM, LANES)
    seq_spec = pl.BlockSpec(blk, lambda i: (i, 0, 0))
    return pl.pallas_call(
        functools.partial(_rwkv_scan_body, tb=tb),
        grid=(t // tb,),
        in_specs=[seq_spec] * 5 + [vec] * 5,
        out_specs=seq_spec,
        out_shape=jax.ShapeDtypeStruct((t, HEAD_DIM, LANES), F32),
        scratch_shapes=[pltpu.VMEM((HEAD_DIM, HEAD_DIM, LANES), F32)] + [pltpu.VMEM(blk, F32)] * 8,
        compiler_params=_params("arbitrary"),
        name="rwkv_scan",
    )(r, k, v, w, a, kk, ka, rk, lng, lnb)


def _ssd_body(xbc_ref, z_ref, dt_ref, dtt_ref, cw_ref, cb_ref, dtb_row_ref, dtb_col_ref,
              a_row_ref, a_col_ref, e_ref, dsk_ref, ng_ref, o_ref, ext_ref, st_ref):
    L = SSM_CHUNK

    @pl.when(pl.program_id(1) == 0)
    def _():
        ext_ref[...] = jnp.zeros_like(ext_ref)
        st_ref[...] = jnp.zeros_like(st_ref)

    xbc = xbc_ref[...]
    tail = ext_ref[...]
    ext_ref[...] = xbc[L - SUBLANES:L, :]
    conv = cb_ref[...] + cw_ref[SSM_CONV - 1:SSM_CONV, :] * xbc
    for s in range(1, SSM_CONV):
        conv = conv + cw_ref[SSM_CONV - 1 - s:SSM_CONV - s, :] * _shift_rows(xbc, s, tail)
    xc = _silu(conv)
    xs = xc[:, :HALF]
    bm = xc[:, HALF:HALF + SSM_GROUPS * SSM_STATE]
    cm = xc[:, HALF + SSM_GROUPS * SSM_STATE:]

    row = lax.broadcasted_iota(jnp.int32, (L, L), 0)
    col = lax.broadcasted_iota(jnp.int32, (L, L), 1)
    lower = row >= col
    dtc = _softplus(dt_ref[...] + dtb_row_ref[...])
    cum_col = _dot(lower.astype(F32), dtc * a_row_ref[...], _HI)
    dtr = _softplus(dtt_ref[0] + dtb_col_ref[...])
    cum_row = _dot(dtr * a_col_ref[...], (row <= col).astype(F32), _HI)
    dtx = _dot(dtc, e_ref[...], _HI)
    ccx = _dot(cum_col, e_ref[...], _HI)
    x = xs * dtx
    ecc = jnp.exp(ccx)
    clast = ccx[L - 1:L, :]
    xd = x * jnp.exp(clast - ccx)
    elast = jnp.exp(clast)
    lane = lax.broadcasted_iota(jnp.int32, (L, LANES), 1)

    ys = []
    for g in range(SSM_GROUPS):
        bg = bm[:, g * SSM_STATE:(g + 1) * SSM_STATE]
        cg = cm[:, g * SSM_STATE:(g + 1) * SSM_STATE].astype(BF16)
        cbm = _dot_nt(cg, bg.astype(BF16))
        bgt = bg.T.astype(BF16)
        for q in range(2):
            p = g * 2 + q
            sl = slice(p * LANES, (p + 1) * LANES)
            st = st_ref[p]
            yp = _dot(cg, st.astype(BF16)) * ecc[:, sl]
            xp = x[:, sl]
            for hh in range(2):
                h = 2 * p + hh
                seg = cum_col[:, h:h + 1] - cum_row[h:h + 1, :]
                dec = jnp.exp(jnp.where(lower, seg, NEG_BIG))
                xh = jnp.where((lane // HEAD_DIM) == hh, xp, 0.0).astype(BF16)
                yp = yp + _dot((cbm * dec).astype(BF16), xh)
            st_ref[p] = st * elast[:, sl] + _dot(bgt, xd[:, sl].astype(BF16))
            ys.append(yp)
    y = jnp.concatenate(ys, axis=1) + xs * dsk_ref[...]
    u = y * _silu(z_ref[...])
    gw = HALF // SSM_GROUPS
    outs = []
    for g in range(SSM_GROUPS):
        ug = u[:, g * gw:(g + 1) * gw]
        outs.append(ug * lax.rsqrt(jnp.mean(ug * ug, axis=-1, keepdims=True) + 1e-5))
    o_ref[...] = (jnp.concatenate(outs, axis=1) * ng_ref[...]).astype(o_ref.dtype)


def _ssd(ps, dtt, cw, cb, dtb_row, dtb_col, a_row, a_col, e, dsk, ng, *, batch, seq):
    L = SSM_CHUNK
    nc = seq // L
    consts = [cw, cb, dtb_row, dtb_col, a_row, a_col, e, dsk, ng]
    return pl.pallas_call(
        _ssd_body,
        grid=(batch, nc),
        in_specs=[
            pl.BlockSpec((L, SSM_XBC), lambda b, c: (b * nc + c, 0)),
            pl.BlockSpec((L, HALF), lambda b, c: (b * nc + c, SSM_XBC // HALF)),
            pl.BlockSpec((L, LANES), lambda b, c: (b * nc + c, (SSM_XBC + HALF) // LANES)),
            pl.BlockSpec((1, N_HEADS, L), lambda b, c: (b, 0, c)),
        ] + [_const_spec(c.shape) for c in consts],
        out_specs=pl.BlockSpec((L, HALF), lambda b, c: (b * nc + c, 0)),
        out_shape=jax.ShapeDtypeStruct((batch * seq, HALF), BF16),
        scratch_shapes=[pltpu.VMEM((SUBLANES, SSM_XBC), F32),
                        pltpu.VMEM((N_HEADS // 2, SSM_STATE, LANES), F32)],
        compiler_params=_params("arbitrary", "arbitrary"),
        name="ssd",
    )(ps, ps, ps, dtt, *consts)


def _mix_out_body(*refs, gated):
    if gated:
        ya_ref, g_ref, yb_ref, x_ref, w_ref, lng_ref, lnb_ref, o_ref = refs
        ya = ya_ref[...] * g_ref[...]
    else:
        ya_ref, yb_ref, x_ref, w_ref, lng_ref, lnb_ref, o_ref = refs
        ya = ya_ref[...]
    mixed = _dot(ya.astype(BF16), w_ref[0:HALF, :]) + _dot(yb_ref[...].astype(BF16), w_ref[HALF:, :])
    o_ref[...] = _layer_norm(ALPHA * x_ref[...] + mixed, lng_ref[...], lnb_ref[...])


def _mix_out(ya, g, yb, x2, w, lng, lnb, *, seq, tm=512):
    m = x2.shape[0]
    tm = min(tm, seq)
    tps = seq // tm
    half = pl.BlockSpec((tm, HALF), lambda i: (i, 0))
    full = pl.BlockSpec((tm, D_MODEL), lambda i: (i, 0))
    gated = g is not None
    acts = [ya, g, yb] if gated else [ya, yb]
    act_specs = [pl.BlockSpec((tm, HALF), lambda i: (i % tps, i // tps)), half, half] if gated else [half, half]
    return pl.pallas_call(
        functools.partial(_mix_out_body, gated=gated),
        grid=(m // tm,),
        in_specs=act_specs + [full, _const_spec(w.shape), _const_spec(lng.shape), _const_spec(lnb.shape)],
        out_specs=full,
        out_shape=jax.ShapeDtypeStruct((m, D_MODEL), F32),
        compiler_params=_params("arbitrary"),
        name="mix_out",
    )(*acts, x2, w, lng, lnb)


def _shift_rows(g, s, tail):
    rolled = pltpu.roll(g, s, 0)
    head = rolled[:SUBLANES]
    row = lax.broadcasted_iota(jnp.int32, head.shape, 0)
    for r in range(s):
        head = jnp.where(row == r, tail[SUBLANES - s + r:SUBLANES - s + r + 1, :], head)
    return jnp.concatenate([head, rolled[SUBLANES:]], axis=0)


def _ffn_body(x_ref, wg_ref, wu_ref, cw_ref, cb_ref, wd_ref, lng_ref, lnb_ref, o_ref,
              xb_ref, acc_ref, act_ref, carry_ref, *, tiles_per_seq, chunk):
    i = pl.program_id(0)
    f = pl.program_id(1)
    nf = pl.num_programs(1)
    tm, tf = act_ref.shape

    @pl.when(f == 0)
    def _():
        xb_ref[...] = x_ref[...].astype(BF16)

    xb = xb_ref[...]
    first = (i % tiles_per_seq) == 0
    for c0 in range(0, tf, chunk):
        cs = slice(c0, min(c0 + chunk, tf))
        gate = _dot(xb, wg_ref[:, cs])
        up = _dot(xb, wu_ref[:, cs])
        tail = jnp.where(first, 0.0, carry_ref[f, :, cs])
        carry_ref[f, :, cs] = gate[tm - SUBLANES:tm, :]
        conv = cb_ref[:, cs] + cw_ref[FFN_CONV - 1:FFN_CONV, cs] * gate
        for s in range(1, FFN_CONV):
            conv = conv + cw_ref[FFN_CONV - 1 - s:FFN_CONV - s, cs] * _shift_rows(gate, s, tail)
        act_ref[:, cs] = (_silu(conv) * up).astype(BF16)
    down = _dot(act_ref[...], wd_ref[...])

    @pl.when(f == 0)
    def _():
        acc_ref[...] = down

    @pl.when(f == nf - 1)
    def _():
        o_ref[...] = _layer_norm(ALPHA * x_ref[...] + acc_ref[...] + down, lng_ref[...], lnb_ref[...])


def _ffn(x2, wg, wu, cw, cb, wd, lng, lnb, *, seq, tm=512, tf=D_FF // 2, chunk=256):
    m = x2.shape[0]
    tm = min(tm, seq)
    nf = D_FF // tf
    assert nf == 2, "the accumulator is written on the first d_ff tile and consumed on the last"
    return pl.pallas_call(
        functools.partial(_ffn_body, tiles_per_seq=seq // tm, chunk=chunk),
        grid=(m // tm, nf),
        in_specs=[
            pl.BlockSpec((tm, D_MODEL), lambda i, f: (i, 0)),
            pl.BlockSpec((D_MODEL, tf), lambda i, f: (0, f)),
            pl.BlockSpec((D_MODEL, tf), lambda i, f: (0, f)),
            pl.BlockSpec((FFN_CONV, tf), lambda i, f: (0, f)),
            pl.BlockSpec((1, tf), lambda i, f: (0, f)),
            pl.BlockSpec((tf, D_MODEL), lambda i, f: (f, 0)),
            _const_spec(lng.shape), _const_spec(lnb.shape),
        ],
        out_specs=pl.BlockSpec((tm, D_MODEL), lambda i, f: (i, 0)),
        out_shape=jax.ShapeDtypeStruct((m, D_MODEL), F32),
        scratch_shapes=[pltpu.VMEM((tm, D_MODEL), BF16), pltpu.VMEM((tm, D_MODEL), F32),
                        pltpu.VMEM((tm, tf), BF16), pltpu.VMEM((nf, SUBLANES, tf), F32)],
        compiler_params=_params("arbitrary", "arbitrary"),
        name="conv_ffn",
    )(x2, wg, wu, cw, cb, wd, lng, lnb)


SB_SCALE = HEAD_DIM ** -0.5
assert math.frexp(SB_SCALE)[0] == 0.5, "folded into q before the bf16 cast, so it must be a power of two"
PAIR = 2 * LANES
N_PAIRS = N_HEADS // 2
L1_SB = 3 * HALF
L1_COLS_PAD = L1_SB + MLA_Q_LORA + MLA_KV_LORA + 2 * LANES


def _store_head_values(v_ref, v, even_head):
    v_ref[:, :HALF] = jnp.where(even_head, v, 0.0).astype(BF16)
    v_ref[:, HALF:] = jnp.where(even_head, 0.0, v).astype(BF16)


def _l1_in_body(x_ref, pos_ref, w_ref, invf_ref, qg_ref, wuq_ref, kvg_ref, wukv_ref,
                qsb_ref, ksb_ref, vsb_ref, q_ref, k_ref, v_ref):
    p = _dot(x_ref[...].astype(BF16), w_ref[...])
    qsb_ref[...] = (p[:, 0:HALF] * SB_SCALE).astype(BF16)
    ksb_ref[...] = p[:, HALF:2 * HALF].astype(BF16)
    lane = lax.broadcasted_iota(jnp.int32, (p.shape[0], HALF), 1)
    even_head = ((lane // HEAD_DIM) & 1) == 0
    _store_head_values(vsb_ref, p[:, 2 * HALF:3 * HALF], even_head)
    c_q = p[:, L1_SB:L1_SB + MLA_Q_LORA]
    c_kv = p[:, L1_SB + MLA_Q_LORA:L1_SB + MLA_Q_LORA + MLA_KV_LORA]
    off = L1_SB + MLA_Q_LORA + MLA_KV_LORA
    kpe = p[:, off:off + LANES]
    kpe_rot = p[:, off + LANES:off + 2 * LANES]
    ang = pos_ref[...].astype(F32) * invf_ref[...]
    cos = jnp.cos(ang)
    sin = jnp.sin(ang)
    kpe = (kpe * cos + kpe_rot * sin).astype(BF16)
    cqn = c_q * lax.rsqrt(jnp.mean(c_q * c_q, axis=-1, keepdims=True) + 1e-6) * qg_ref[...]
    q = _dot(cqn.astype(BF16), wuq_ref[...])
    ckn = c_kv * lax.rsqrt(jnp.mean(c_kv * c_kv, axis=-1, keepdims=True) + 1e-6) * kvg_ref[...]
    kv = _dot(ckn.astype(BF16), wukv_ref[...])
    for pr in range(N_PAIRS):
        q_ref[:, pr * PAIR:pr * PAIR + LANES] = q[:, pr * PAIR:pr * PAIR + LANES].astype(BF16)
        q_pe = q[:, pr * PAIR + LANES:(pr + 1) * PAIR]
        q_rot = q[:, N_PAIRS * PAIR + pr * LANES:N_PAIRS * PAIR + (pr + 1) * LANES]
        q_ref[:, pr * PAIR + LANES:(pr + 1) * PAIR] = (q_pe * cos + q_rot * sin).astype(BF16)
        k_ref[:, pr * PAIR:pr * PAIR + LANES] = kv[:, pr * LANES:(pr + 1) * LANES].astype(BF16)
        k_ref[:, pr * PAIR + LANES:(pr + 1) * PAIR] = kpe
    vt = kv[:, HALF:].T
    chan = lax.broadcasted_iota(jnp.int32, vt.shape, 0)
    even_chan = ((chan // MLA_V) & 1) == 0
    v_ref[:HALF, :] = jnp.where(even_chan, vt, 0.0).astype(BF16)
    v_ref[HALF:, :] = jnp.where(even_chan, 0.0, vt).astype(BF16)


def _l1_in(x2, pos2, w, invf, qg, wuq, kvg, wukv, *, seq, tm=512):
    m = x2.shape[0]
    tm = min(tm, seq)
    tps = seq // tm
    consts = [w, invf, qg, wuq, kvg, wukv]
    v_t = pl.BlockSpec((2 * HALF, tm), lambda i: (i // tps, i % tps))
    half = pl.BlockSpec((tm, HALF), lambda i: (i, 0))
    cat = pl.BlockSpec((tm, N_PAIRS * PAIR), lambda i: (i, 0))
    assert N_PAIRS * PAIR == 2 * HALF
    sd = jax.ShapeDtypeStruct
    return pl.pallas_call(
        _l1_in_body,
        grid=(m // tm,),
        in_specs=[pl.BlockSpec((tm, D_MODEL), lambda i: (i, 0)), pl.BlockSpec((tm, 1), lambda i: (i, 0))]
        + [_const_spec(c.shape) for c in consts],
        out_specs=[half, half, cat, cat, cat, v_t],
        out_shape=[sd((m, HALF), BF16)] * 2 + [sd((m, 2 * HALF), BF16)] * 3 + [sd((m // seq * 2 * HALF, seq), BF16)],
        compiler_params=_params("arbitrary"),
        name="l1_in",
    )(x2, pos2, *consts)


ATT_BLOCK = 256


def _stack_heads(q_ref, qs_ref, masks, width):
    tq = ATT_BLOCK
    for p in range(N_PAIRS):
        qp = q_ref[:, p * width:(p + 1) * width]
        for hh in range(2):
            qs_ref[p, hh * tq:(hh + 1) * tq, :] = jnp.where(masks[hh], qp, jnp.zeros_like(qp))


def _head_values(v_ref, start, p):
    return [v_ref[pl.ds(start, ATT_BLOCK), hh * HALF + p * LANES:hh * HALF + (p + 1) * LANES] for hh in range(2)]


def _sb_body(q_ref, k_ref, v_ref, o_ref, qs_ref, acc_ref, s_ref, w_ref):
    qi = pl.program_id(1)
    tq = ATT_BLOCK
    lane = lax.broadcasted_iota(jnp.int32, (tq, LANES), 1)
    _stack_heads(q_ref, qs_ref, [(lane // HEAD_DIM) == hh for hh in range(2)], LANES)
    acc_ref[...] = jnp.zeros_like(acc_ref)
    row = lax.broadcasted_iota(jnp.int32, (2 * tq, tq), 0) & (tq - 1)
    col = lax.broadcasted_iota(jnp.int32, (2 * tq, tq), 1)
    strict = col < row
    later = ((lax.broadcasted_iota(jnp.int32, (2 * tq, tq), 0) & (tq - 1))
             > lax.broadcasted_iota(jnp.int32, (2 * tq, tq), 1)).astype(BF16)

    def scores(kj, slot):
        start = pl.multiple_of(kj * tq, tq)
        for p in range(N_PAIRS):
            s_ref[slot, p] = _dot_nt(qs_ref[p], k_ref[pl.ds(start, tq), p * LANES:(p + 1) * LANES])

    def weights(slot, runs, diagonal):
        new_runs = []
        for p in range(N_PAIRS):
            z = s_ref[slot, p]
            log_beta = jnp.minimum(z, 0.0) - jnp.log(1.0 + jnp.exp(-jnp.abs(z)))
            log_keep = log_beta - z
            if diagonal:
                log_keep = jnp.where(strict, log_keep, 0.0)
            hi = log_keep.astype(BF16)
            lo = (log_keep - hi.astype(F32)).astype(BF16)
            suffix = _dot(jnp.concatenate([hi, lo], axis=1), later)
            att = jnp.exp(log_beta + suffix + runs[p])
            if diagonal:
                att = jnp.where(strict, att, 0.0)
            w_ref[slot, p] = att.astype(BF16)
            new_runs.append(runs[p] + jnp.sum(log_keep, axis=1, keepdims=True))
        return tuple(new_runs)

    def accumulate(kj, slot):
        start = pl.multiple_of(kj * tq, tq)
        for p in range(N_PAIRS):
            vh = _head_values(v_ref, start, p)
            w = w_ref[slot, p]
            acc_ref[p] += jnp.concatenate([_dot(w[:tq], vh[0]), _dot(w[tq:], vh[1])], axis=0)

    scores(qi, 0)
    scores(jnp.maximum(qi - 1, 0), 1)
    runs = weights(0, tuple(jnp.zeros((2 * tq, 1), F32) for _ in range(N_PAIRS)), True)

    def trip(i, runs):
        slot = i & 1
        accumulate(qi - i + 1, 1 - slot)
        runs = weights(slot, runs, False)
        scores(jnp.maximum(qi - i - 1, 0), 1 - slot)
        return runs

    lax.fori_loop(1, qi + 1, trip, runs)
    accumulate(0, qi & 1)
    for p in range(N_PAIRS):
        o_ref[:, p * LANES:(p + 1) * LANES] = (acc_ref[p, :tq] + acc_ref[p, tq:]).astype(o_ref.dtype)


def _sb_attention(q, k, v, *, batch, seq):
    tq = ATT_BLOCK
    nq = seq // tq
    return pl.pallas_call(
        _sb_body,
        grid=(batch, nq),
        in_specs=[
            pl.BlockSpec((tq, HALF), lambda b, i: (b * nq + i, 0)),
            pl.BlockSpec((seq, HALF), lambda b, i: (b, 0)),
            pl.BlockSpec((seq, 2 * HALF), lambda b, i: (b, 0)),
        ],
        out_specs=pl.BlockSpec((tq, HALF), lambda b, i: (b * nq + i, 0)),
        out_shape=jax.ShapeDtypeStruct((batch * seq, HALF), BF16),
        scratch_shapes=[pltpu.VMEM((N_PAIRS, 2 * tq, LANES), BF16), pltpu.VMEM((N_PAIRS, 2 * tq, LANES), F32),
                        pltpu.VMEM((2, N_PAIRS, 2 * tq, tq), F32), pltpu.VMEM((2, N_PAIRS, 2 * tq, tq), BF16)],
        compiler_params=_params("arbitrary", "arbitrary"),
        name="sb_attention",
    )(q, k, v)


def _mla_body(q_ref, k_ref, vt_ref, o_ref, qs_ref, acc_ref, s_ref, w_ref):
    qi = pl.program_id(1)
    tq = ATT_BLOCK
    scale = (MLA_NOPE + MLA_ROPE) ** -0.5
    lane2 = lax.broadcasted_iota(jnp.int32, (tq, PAIR), 1)
    masks = [((lane2 >= hh * MLA_NOPE) & (lane2 < (hh + 1) * MLA_NOPE))
             | ((lane2 >= LANES + hh * MLA_ROPE) & (lane2 < LANES + (hh + 1) * MLA_ROPE)) for hh in range(2)]
    _stack_heads(q_ref, qs_ref, masks, PAIR)
    acc_ref[...] = jnp.zeros_like(acc_ref)
    key = lax.broadcasted_iota(jnp.int32, (tq, 2 * tq), 0)
    qry = lax.broadcasted_iota(jnp.int32, (tq, 2 * tq), 1) & (tq - 1)
    causal = key <= qry

    def scores(kj, slot):
        start = pl.multiple_of(kj * tq, tq)
        for p in range(N_PAIRS):
            k = k_ref[pl.ds(start, tq), p * PAIR:(p + 1) * PAIR]
            s_ref[slot, p] = _dot_nt(k, qs_ref[p]) * scale

    def values(kj, slot, p):
        start = pl.multiple_of(kj * tq, tq)
        vt = [vt_ref[hh * HALF + p * LANES:hh * HALF + (p + 1) * LANES, pl.ds(start, tq)] for hh in range(2)]
        w = w_ref[slot, p]
        return jnp.concatenate([_dot(vt[0], w[:, :tq]), _dot(vt[1], w[:, tq:])], axis=1)

    def weights(slot, carry, diagonal, prev=None):
        pvs = [values(prev[0], prev[1], p) for p in range(N_PAIRS)] if prev is not None else None
        new = []
        for p in range(N_PAIRS):
            m, l = carry[p]
            s = s_ref[slot, p]
            if diagonal:
                s = jnp.where(causal, s, NEG_BIG)
            m_new = jnp.maximum(m, jnp.max(s, axis=0, keepdims=True))
            corr = jnp.exp(m - m_new)
            pexp = jnp.exp(s - m_new)
            l = l * corr + jnp.sum(pexp, axis=0, keepdims=True)
            w_ref[slot, p] = pexp.astype(BF16)
            if prev is not None:
                acc_ref[p] = (acc_ref[p] + pvs[p]) * corr
            new.append((m_new, l))
        return tuple(new)

    init = tuple((jnp.full((1, 2 * tq), NEG_BIG, F32), jnp.zeros((1, 2 * tq), F32)) for _ in range(N_PAIRS))
    scores(qi, 0)
    scores(0, 1)
    carry = weights(0, init, True)

    def trip(i, carry):
        slot = i & 1
        carry = weights(slot, carry, False, prev=(jnp.where(i == 1, qi, i - 2), 1 - slot))
        scores(jnp.minimum(i, jnp.maximum(qi - 1, 0)), 1 - slot)
        return carry

    carry = lax.fori_loop(1, qi + 1, trip, carry)
    last = (jnp.where(qi == 0, 0, qi - 1), qi & 1)
    for p in range(N_PAIRS):
        acc_ref[p] += values(last[0], last[1], p)
    for p in range(N_PAIRS):
        out = acc_ref[p] / carry[p][1]
        o_ref[:, p * LANES:(p + 1) * LANES] = (out[:, :tq] + out[:, tq:]).T.astype(o_ref.dtype)


def _mla_attention(q, k, v, *, batch, seq):
    tq = ATT_BLOCK
    nq = seq // tq
    return pl.pallas_call(
        _mla_body,
        grid=(batch, nq),
        in_specs=[
            pl.BlockSpec((tq, N_PAIRS * PAIR), lambda b, i: (b * nq + i, 0)),
            pl.BlockSpec((seq, N_PAIRS * PAIR), lambda b, i: (b, 0)),
            pl.BlockSpec((2 * HALF, seq), lambda b, i: (b, 0)),
        ],
        out_specs=pl.BlockSpec((tq, HALF), lambda b, i: (b * nq + i, 0)),
        out_shape=jax.ShapeDtypeStruct((batch * seq, HALF), BF16),
        scratch_shapes=[pltpu.VMEM((N_PAIRS, 2 * tq, PAIR), BF16), pltpu.VMEM((N_PAIRS, LANES, 2 * tq), F32),
                        pltpu.VMEM((2, N_PAIRS, tq, 2 * tq), F32), pltpu.VMEM((2, N_PAIRS, tq, 2 * tq), BF16)],
        compiler_params=_params("arbitrary", "arbitrary"),
        name="mla_attention",
    )(q, k, v)


def _row(v):
    return v.reshape(1, -1).astype(F32)


def _chain_layout(v):
    lane = jnp.arange(LANES)
    head = (2 * (lane % (LANES // 2)) + lane // (LANES // 2)) % N_HEADS
    return v.reshape(N_HEADS, HEAD_DIM).T[:, head].astype(F32)


def _rope_rotate_cols(w):
    half = MLA_ROPE // 2
    return jnp.concatenate([-w[..., half:], w[..., :half]], axis=-1)


def _layer0_mixer(x2, batch, seq, l0_w_in, rwkv_mix, rwkv_w0, rwkv_w2, rwkv_a0, rwkv_a2, rwkv_g2,
                  rwkv_k_k, rwkv_k_a, rwkv_r_k, rwkv_ln_g, rwkv_ln_b, ssm_conv_w, ssm_conv_b,
                  ssm_dt_bias, ssm_a_log, ssm_d, ssm_norm_g):
    w_r = l0_w_in[:, :RWKV_COLS].astype(BF16)
    zeros = jnp.zeros((RWKV_DECAY_LORA, HALF), F32)
    wl = jnp.concatenate([jnp.concatenate([rwkv_w2, zeros], axis=1),
                          jnp.concatenate([zeros, rwkv_a2], axis=1)], axis=0).astype(BF16)
    w0a0 = _row(jnp.concatenate([rwkv_w0, rwkv_a0]))
    *rkvwa, gate = _rwkv_in(x2, w_r, _row(rwkv_mix), wl, w0a0, rwkv_g2.astype(BF16), seq=seq)
    as_tiles = lambda t: t.reshape(seq, HEAD_DIM, LANES)
    y = _rwkv_scan(*[as_tiles(t) for t in rkvwa], _chain_layout(rwkv_k_k), _chain_layout(rwkv_k_a),
                   _chain_layout(rwkv_r_k.reshape(-1)), _chain_layout(rwkv_ln_g), _chain_layout(rwkv_ln_b))
    y_a = y.reshape(seq, batch * HALF)

    w_s = l0_w_in[:, RWKV_COLS:]
    pad = jnp.zeros((D_MODEL, LANES - N_HEADS), F32)
    w_s = jnp.concatenate([w_s[:, HALF:HALF + SSM_XBC], w_s[:, :HALF], w_s[:, HALF + SSM_XBC:], pad], axis=1)
    ps = _proj(x2, w_s.astype(BF16))
    dt_raw = ps[:, SSM_XBC + HALF:SSM_XBC + HALF + N_HEADS]
    dtt = dt_raw.reshape(batch, seq, N_HEADS).transpose(0, 2, 1)
    a_neg = -jnp.exp(ssm_a_log.astype(F32))
    lane_pad = jnp.zeros((LANES - N_HEADS,), F32)
    expand = jnp.concatenate([jnp.repeat(jnp.eye(N_HEADS, dtype=F32), HEAD_DIM, axis=1),
                              jnp.zeros((LANES - N_HEADS, HALF), F32)], axis=0)
    y_b = _ssd(ps, dtt, ssm_conv_w.astype(F32), _row(ssm_conv_b),
               _row(jnp.concatenate([ssm_dt_bias, lane_pad])), ssm_dt_bias.reshape(N_HEADS, 1).astype(F32),
               _row(jnp.concatenate([a_neg, lane_pad])), a_neg.reshape(N_HEADS, 1),
               expand, _row(jnp.repeat(ssm_d, HEAD_DIM)), _row(ssm_norm_g), batch=batch, seq=seq)
    return y_a, gate, y_b


def _layer1_mixer(h2, positions, batch, seq, l1_w_in, mla_q_norm_g, mla_w_uq, mla_kv_norm_g, mla_w_ukv):
    off = L1_SB + MLA_Q_LORA + MLA_KV_LORA
    w_kpe = l1_w_in[:, off:off + MLA_ROPE]
    lane_zeros = jnp.zeros((D_MODEL, LANES - 2 * MLA_ROPE), F32)
    w_kpe_blk = jnp.concatenate([w_kpe, w_kpe, lane_zeros], axis=1)
    w_kpe_rot = _rope_rotate_cols(w_kpe)
    w_kpe_rot_blk = jnp.concatenate([w_kpe_rot, w_kpe_rot, lane_zeros], axis=1)
    w_in = jnp.concatenate([l1_w_in[:, :off], w_kpe_blk, w_kpe_rot_blk], axis=1).astype(BF16)

    inv_freq = 1.0 / (ROPE_THETA ** (jnp.arange(0, MLA_ROPE, 2, dtype=F32) / MLA_ROPE))
    invf = jnp.concatenate([jnp.tile(inv_freq, 4), jnp.zeros((LANES - 2 * MLA_ROPE,), F32)]).reshape(1, LANES)

    wq = mla_w_uq.reshape(MLA_Q_LORA, N_HEADS, MLA_NOPE + MLA_ROPE)
    wq_nope = wq[:, :, :MLA_NOPE].reshape(MLA_Q_LORA, N_PAIRS, 2 * MLA_NOPE)
    wq_pe = wq[:, :, MLA_NOPE:]
    lz = jnp.zeros((MLA_Q_LORA, N_PAIRS, LANES - 2 * MLA_ROPE), F32)
    wq_pe_blk = jnp.concatenate([wq_pe.reshape(MLA_Q_LORA, N_PAIRS, 2 * MLA_ROPE), lz], axis=2)
    wq_rot_blk = jnp.concatenate([_rope_rotate_cols(wq_pe).reshape(MLA_Q_LORA, N_PAIRS, 2 * MLA_ROPE), lz], axis=2)
    wuq = jnp.concatenate([jnp.concatenate([wq_nope, wq_pe_blk], axis=2).reshape(MLA_Q_LORA, N_PAIRS * PAIR),
                           wq_rot_blk.reshape(MLA_Q_LORA, N_PAIRS * LANES)], axis=1).astype(BF16)

    wkv = mla_w_ukv.reshape(MLA_KV_LORA, N_HEADS, MLA_NOPE + MLA_V)
    wukv = jnp.concatenate([wkv[:, :, :MLA_NOPE].reshape(MLA_KV_LORA, HALF),
                            wkv[:, :, MLA_NOPE:].reshape(MLA_KV_LORA, HALF)], axis=1).astype(BF16)

    qsb, ksb, vsb, q, k, v = _l1_in(h2, positions.reshape(batch * seq, 1), w_in, invf,
                                    _row(mla_q_norm_g), wuq, _row(mla_kv_norm_g), wukv, seq=seq)
    y_c = _sb_attention(qsb, ksb, vsb, batch=batch, seq=seq)
    y_d = _mla_attention(q, k, v, batch=batch, seq=seq)
    return y_c, y_d


def _ffn_layer(h2, seq, w_up, conv_w, conv_b, w_down, ln_g, ln_b):
    return _ffn(h2, w_up[:, :D_FF].astype(BF16), w_up[:, D_FF:].astype(BF16), conv_w.astype(F32),
                _row(conv_b), w_down.astype(BF16), _row(ln_g), _row(ln_b), seq=seq)


def kernel(x, positions, l0_w_in, rwkv_mix, rwkv_w0, rwkv_w2, rwkv_a0, rwkv_a2, rwkv_g2, rwkv_k_k, rwkv_k_a, rwkv_r_k, rwkv_ln_g, rwkv_ln_b, ssm_conv_w, ssm_conv_b, ssm_dt_bias, ssm_a_log, ssm_d, ssm_norm_g, l0_w_out, l0_ln1_g, l0_ln1_b, ffn0_w_up, ffn0_conv_w, ffn0_conv_b, ffn0_w_down, l0_ln2_g, l0_ln2_b, l1_w_in, mla_q_norm_g, mla_w_uq, mla_kv_norm_g, mla_w_ukv, l1_w_out, l1_ln1_g, l1_ln1_b, ffn1_w_up, ffn1_conv_w, ffn1_conv_b, ffn1_w_down, l1_ln2_g, l1_ln2_b):
    batch, seq, _ = x.shape
    assert batch * N_HEADS == LANES, "the RWKV scan maps batch*heads onto the lane axis"
    x2 = x.reshape(batch * seq, D_MODEL).astype(F32)

    y_a, gate, y_b = _layer0_mixer(x2, batch, seq, l0_w_in, rwkv_mix, rwkv_w0, rwkv_w2, rwkv_a0, rwkv_a2,
                                   rwkv_g2, rwkv_k_k, rwkv_k_a, rwkv_r_k, rwkv_ln_g, rwkv_ln_b, ssm_conv_w,
                                   ssm_conv_b, ssm_dt_bias, ssm_a_log, ssm_d, ssm_norm_g)
    h = _mix_out(y_a, gate, y_b, x2, l0_w_out.astype(BF16), _row(l0_ln1_g), _row(l0_ln1_b), seq=seq)
    h = _ffn_layer(h, seq, ffn0_w_up, ffn0_conv_w, ffn0_conv_b, ffn0_w_down, l0_ln2_g, l0_ln2_b)

    y_c, y_d = _layer1_mixer(h, positions, batch, seq, l1_w_in, mla_q_norm_g, mla_w_uq, mla_kv_norm_g, mla_w_ukv)
    h = _mix_out(y_c, None, y_d, h, l1_w_out.astype(BF16), _row(l1_ln1_g), _row(l1_ln1_b), seq=seq)
    h = _ffn_layer(h, seq, ffn1_w_up, ffn1_conv_w, ffn1_conv_b, ffn1_w_down, l1_ln2_g, l1_ln2_b)
    return h.reshape(batch, seq, D_MODEL).astype(x.dtype)
```

```python
import functools
import math

import jax
import jax.numpy as jnp
from jax import lax
from jax.experimental import pallas as pl
from jax.experimental.pallas import tpu as pltpu

F32 = jnp.float32
BF16 = jnp.bfloat16

D_MODEL = 1024
HEAD_DIM = 64
N_HEADS = 8
HALF = N_HEADS * HEAD_DIM
RWKV_DECAY_LORA = 64
RWKV_A_LORA = 64
RWKV_GATE_LORA = 128
RWKV_GN_EPS = 64e-5
RWKV_COLS = 3 * HALF + RWKV_DECAY_LORA + RWKV_A_LORA + RWKV_GATE_LORA
SSM_GROUPS = 2
SSM_STATE = 128
SSM_CONV = 4
SSM_CHUNK = 128
SSM_XBC = HALF + 2 * SSM_GROUPS * SSM_STATE
SSM_COLS = HALF + SSM_XBC + N_HEADS
SSM_COLS_PAD = SSM_XBC + HALF + 128
MLA_NOPE = 64
MLA_ROPE = 32
MLA_V = 64
MLA_Q_LORA = 256
MLA_KV_LORA = 128
ROPE_THETA = 10000.0
D_FF = 2816
FFN_CONV = 3
DEPTH = 2
ALPHA = (2 * DEPTH) ** 0.25
LN_EPS = 1e-5

LANES = 128
SUBLANES = 8
VMEM_LIMIT = 56 * 1024 * 1024
NEG_BIG = -1e30

_HI = lax.Precision.HIGHEST


def _params(*sem):
    return pltpu.CompilerParams(dimension_semantics=sem, vmem_limit_bytes=VMEM_LIMIT)


def _dot(a, b, precision=None):
    return jnp.dot(a, b, preferred_element_type=F32, precision=precision)


def _dot_nt(a, b):
    return lax.dot_general(a, b, (((1,), (1,)), ((), ())), preferred_element_type=F32)


def _sigmoid(x):
    return 1.0 / (1.0 + jnp.exp(-x))


def _softplus(x):
    return jnp.maximum(x, 0.0) + jnp.log1p(jnp.exp(-jnp.abs(x)))


def _silu(x):
    return x * _sigmoid(x)


def _layer_norm(v, g, b):
    mu = jnp.mean(v, axis=-1, keepdims=True)
    d = v - mu
    var = jnp.mean(d * d, axis=-1, keepdims=True)
    return d * lax.rsqrt(var + LN_EPS) * g + b


def _const_spec(shape):
    nd = len(shape)
    return pl.BlockSpec(shape, lambda *_: (0,) * nd)


def _rwkv_in_body(x_ref, w_ref, mix_ref, wl_ref, w0a0_ref, g2_ref, r_ref, k_ref, v_ref, dec_ref, a_ref, g_ref,
                  carry_ref, *, tiles_per_seq):
    i = pl.program_id(0)
    p = _dot(x_ref[...].astype(BF16), w_ref[...])
    tm = p.shape[0]
    first = (i % tiles_per_seq) == 0
    prev_row = jnp.where(first, 0.0, carry_ref[SUBLANES - 1:SUBLANES, :])
    row = lax.broadcasted_iota(jnp.int32, p.shape, 0)
    prev = jnp.where(row == 0, prev_row, pltpu.roll(p, 1, 0))
    carry_ref[...] = p[tm - SUBLANES:tm, :]
    pm = p + (prev - p) * mix_ref[...]
    lo = pm[:, 3 * HALF:3 * HALF + LANES]
    lane = lax.broadcasted_iota(jnp.int32, lo.shape, 1)
    lo = jnp.where(lane < RWKV_DECAY_LORA, jnp.tanh(lo), lo)
    wa = _dot(lo.astype(BF16), wl_ref[...]) + w0a0_ref[...]
    log_w = -_softplus(-wa[:, :HALF]) - 0.5
    r_ref[...] = pm[:, 0:HALF]
    k_ref[...] = pm[:, HALF:2 * HALF]
    v_ref[...] = pm[:, 2 * HALF:3 * HALF]
    dec_ref[...] = jnp.exp(-jnp.exp(log_w))
    a_ref[...] = _sigmoid(wa[:, HALF:])
    g_lo = _sigmoid(pm[:, 3 * HALF + LANES:])
    g_ref[...] = _dot(g_lo.astype(BF16), g2_ref[...])


def _rwkv_in(x2, w_r, mix, wl, w0a0, g2, *, seq, tm=256):
    m = x2.shape[0]
    tps = seq // tm
    batch = m // seq
    body = functools.partial(_rwkv_in_body, tiles_per_seq=tps)
    time_major = pl.BlockSpec((tm, HALF), lambda i: (i % tps, i // tps))
    return pl.pallas_call(
        body,
        grid=(m // tm,),
        in_specs=[
            pl.BlockSpec((tm, D_MODEL), lambda i: (i, 0)),
            _const_spec(w_r.shape), _const_spec(mix.shape), _const_spec(wl.shape),
            _const_spec(w0a0.shape), _const_spec(g2.shape),
        ],
        out_specs=[time_major] * 5 + [pl.BlockSpec((tm, HALF), lambda i: (i, 0))],
        out_shape=[jax.ShapeDtypeStruct((seq, batch * HALF), F32)] * 5 + [jax.ShapeDtypeStruct((m, HALF), F32)],
        scratch_shapes=[pltpu.VMEM((SUBLANES, RWKV_COLS), F32)],
        compiler_params=_params("arbitrary"),
        name="rwkv_in",
    )(x2, w_r, mix, wl, w0a0, g2)


def _proj_body(x_ref, w_ref, o_ref):
    o_ref[...] = _dot(x_ref[...].astype(BF16), w_ref[...])


def _proj(x2, w, *, tm=512):
    m, k = x2.shape
    n = w.shape[1]
    return pl.pallas_call(
        _proj_body,
        grid=(m // tm,),
        in_specs=[pl.BlockSpec((tm, k), lambda i: (i, 0)), _const_spec(w.shape)],
        out_specs=pl.BlockSpec((tm, n), lambda i: (i, 0)),
        out_shape=jax.ShapeDtypeStruct((m, n), F32),
        compiler_params=_params("arbitrary"),
        name="proj",
    )(x2, w)


SCAN_UNROLL = 16


def _to_chain_lanes(z):
    zt = z.T
    return jnp.concatenate([zt[:HEAD_DIM], zt[HEAD_DIM:]], axis=1)


def _from_chain_lanes(y):
    return jnp.concatenate([y[:, :LANES // 2], y[:, LANES // 2:]], axis=0).T


def _rwkv_scan_body(r_ref, k_ref, v_ref, w_ref, a_ref, kk_ref, ka_ref, rk_ref, lng_ref, lnb_ref, o_ref,
                    h_ref, rs_ref, ws_ref, vs_ref, av_ref, bv_ref, k2_ref, bon_ref, ys_ref, *, tb):
    @pl.when(pl.program_id(0) == 0)
    def _():
        h_ref[...] = jnp.zeros_like(h_ref)

    for t in range(tb):
        r = _to_chain_lanes(r_ref[t])
        k = _to_chain_lanes(k_ref[t])
        v = _to_chain_lanes(v_ref[t])
        a = _to_chain_lanes(a_ref[t])
        kk = k * kk_ref[...]
        nrm = jnp.sqrt(jnp.sum(kk * kk, axis=0, keepdims=True))
        kk = kk / jnp.maximum(nrm, 1e-12)
        k2 = k * (1.0 + (a - 1.0) * ka_ref[...])
        rs_ref[t] = r
        vs_ref[t] = v
        ws_ref[t] = _to_chain_lanes(w_ref[t])
        av_ref[t] = -kk
        bv_ref[t] = kk * a
        k2_ref[t] = k2
        bon_ref[t] = jnp.sum(r * k2 * rk_ref[...], axis=0, keepdims=True) * v

    def step(t, carry):
        zero = jnp.zeros((HEAD_DIM, LANES), F32)

        def read(j, u):
            return u + h_ref[j] * av_ref[t, pl.ds(j, 1), :]

        u = lax.fori_loop(0, HEAD_DIM, read, zero, unroll=SCAN_UNROLL)
        vt = vs_ref[t]

        def update(j, y):
            wj = ws_ref[t, pl.ds(j, 1), :]
            rj = rs_ref[t, pl.ds(j, 1), :]
            bj = bv_ref[t, pl.ds(j, 1), :]
            kj = k2_ref[t, pl.ds(j, 1), :]
            hn = h_ref[j] * wj + u * bj + vt * kj
            h_ref[j] = hn
            return y + hn * rj

        ys_ref[t] = lax.fori_loop(0, HEAD_DIM, update, zero, unroll=SCAN_UNROLL)
        return carry

    lax.fori_loop(0, tb, step, 0)

    for t in range(tb):
        y = ys_ref[t]
        mu = jnp.mean(y, axis=0, keepdims=True)
        d = y - mu
        var = jnp.mean(d * d, axis=0, keepdims=True)
        y = d * lax.rsqrt(var + RWKV_GN_EPS) * lng_ref[...] + lnb_ref[...] + bon_ref[t]
        o_ref[t] = _from_chain_lanes(y)


def _rwkv_scan(r, k, v, w, a, kk, ka, rk, lng, lnb, *, tb=16):
    t = r.shape[0]
    vec = pl.BlockSpec((HEAD_DIM, LANES), lambda i: (0, 0))
    blk = (tb, HEAD_DIM, LANES)
    seq_spec = pl.BlockSpec(blk, lambda i: (i, 0, 0))
    return pl.pallas_call(
        functools.partial(_rwkv_scan_body, tb=tb),
        grid=(t // tb,),
        in_specs=[seq_spec] * 5 + [vec] * 5,
        out_specs=seq_spec,
        out_shape=jax.ShapeDtypeStruct((t, HEAD_DIM, LANES), F32),
        scratch_shapes=[pltpu.VMEM((HEAD_DIM, HEAD_DIM, LANES), F32)] + [pltpu.VMEM(blk, F32)] * 8,
        compiler_params=_params("arbitrary"),
        name="rwkv_scan",
    )(r, k, v, w, a, kk, ka, rk, lng, lnb)


def _ssd_body(xbc_ref, z_ref, dt_ref, dtt_ref, cw_ref, cb_ref, dtb_row_ref, dtb_col_ref,
              a_row_ref, a_col_ref, e_ref, dsk_ref, ng_ref, o_ref, ext_ref, st_ref):
    L = SSM_CHUNK

    @pl.when(pl.program_id(1) == 0)
    def _():
        ext_ref[...] = jnp.zeros_like(ext_ref)
        st_ref[...] = jnp.zeros_like(st_ref)

    xbc = xbc_ref[...]
    tail = ext_ref[...]
    ext_ref[...] = xbc[L - SUBLANES:L, :]
    conv = cb_ref[...] + cw_ref[SSM_CONV - 1:SSM_CONV, :] * xbc
    for s in range(1, SSM_CONV):
        conv = conv + cw_ref[SSM_CONV - 1 - s:SSM_CONV - s, :] * _shift_rows(xbc, s, tail)
    xc = _silu(conv)
    xs = xc[:, :HALF]
    bm = xc[:, HALF:HALF + SSM_GROUPS * SSM_STATE]
    cm = xc[:, HALF + SSM_GROUPS * SSM_STATE:]

    row = lax.broadcasted_iota(jnp.int32, (L, L), 0)
    col = lax.broadcasted_iota(jnp.int32, (L, L), 1)
    lower = row >= col
    dtc = _softplus(dt_ref[...] + dtb_row_ref[...])
    cum_col = _dot(lower.astype(F32), dtc * a_row_ref[...], _HI)
    dtr = _softplus(dtt_ref[0] + dtb_col_ref[...])
    cum_row = _dot(dtr * a_col_ref[...], (row <= col).astype(F32), _HI)
    dtx = _dot(dtc, e_ref[...], _HI)
    ccx = _dot(cum_col, e_ref[...], _HI)
    x = xs * dtx
    ecc = jnp.exp(ccx)
    clast = ccx[L - 1:L, :]
    xd = x * jnp.exp(clast - ccx)
    elast = jnp.exp(clast)
    lane = lax.broadcasted_iota(jnp.int32, (L, LANES), 1)

    ys = []
    for g in range(SSM_GROUPS):
        bg = bm[:, g * SSM_STATE:(g + 1) * SSM_STATE]
        cg = cm[:, g * SSM_STATE:(g + 1) * SSM_STATE].astype(BF16)
        cbm = _dot_nt(cg, bg.astype(BF16))
        bgt = bg.T.astype(BF16)
        for q in range(2):
            p = g * 2 + q
            sl = slice(p * LANES, (p + 1) * LANES)
            st = st_ref[p]
            yp = _dot(cg, st.astype(BF16)) * ecc[:, sl]
            xp = x[:, sl]
            for hh in range(2):
                h = 2 * p + hh
                seg = cum_col[:, h:h + 1] - cum_row[h:h + 1, :]
                dec = jnp.exp(jnp.where(lower, seg, NEG_BIG))
                xh = jnp.where((lane // HEAD_DIM) == hh, xp, 0.0).astype(BF16)
                yp = yp + _dot((cbm * dec).astype(BF16), xh)
            st_ref[p] = st * elast[:, sl] + _dot(bgt, xd[:, sl].astype(BF16))
            ys.append(yp)
    y = jnp.concatenate(ys, axis=1) + xs * dsk_ref[...]
    u = y * _silu(z_ref[...])
    gw = HALF // SSM_GROUPS
    outs = []
    for g in range(SSM_GROUPS):
        ug = u[:, g * gw:(g + 1) * gw]
        outs.append(ug * lax.rsqrt(jnp.mean(ug * ug, axis=-1, keepdims=True) + 1e-5))
    o_ref[...] = (jnp.concatenate(outs, axis=1) * ng_ref[...]).astype(o_ref.dtype)


def _ssd(ps, dtt, cw, cb, dtb_row, dtb_col, a_row, a_col, e, dsk, ng, *, batch, seq):
    L = SSM_CHUNK
    nc = seq // L
    consts = [cw, cb, dtb_row, dtb_col, a_row, a_col, e, dsk, ng]
    return pl.pallas_call(
        _ssd_body,
        grid=(batch, nc),
        in_specs=[
            pl.BlockSpec((L, SSM_XBC), lambda b, c: (b * nc + c, 0)),
            pl.BlockSpec((L, HALF), lambda b, c: (b * nc + c, SSM_XBC // HALF)),
            pl.BlockSpec((L, LANES), lambda b, c: (b * nc + c, (SSM_XBC + HALF) // LANES)),
            pl.BlockSpec((1, N_HEADS, L), lambda b, c: (b, 0, c)),
        ] + [_const_spec(c.shape) for c in consts],
        out_specs=pl.BlockSpec((L, HALF), lambda b, c: (b * nc + c, 0)),
        out_shape=jax.ShapeDtypeStruct((batch * seq, HALF), BF16),
        scratch_shapes=[pltpu.VMEM((SUBLANES, SSM_XBC), F32),
                        pltpu.VMEM((N_HEADS // 2, SSM_STATE, LANES), F32)],
        compiler_params=_params("arbitrary", "arbitrary"),
        name="ssd",
    )(ps, ps, ps, dtt, *consts)


def _mix_out_body(*refs, gated):
    if gated:
        ya_ref, g_ref, yb_ref, x_ref, w_ref, lng_ref, lnb_ref, o_ref = refs
        ya = ya_ref[...] * g_ref[...]
    else:
        ya_ref, yb_ref, x_ref, w_ref, lng_ref, lnb_ref, o_ref = refs
        ya = ya_ref[...]
    mixed = _dot(ya.astype(BF16), w_ref[0:HALF, :]) + _dot(yb_ref[...].astype(BF16), w_ref[HALF:, :])
    o_ref[...] = _layer_norm(ALPHA * x_ref[...] + mixed, lng_ref[...], lnb_ref[...])


def _mix_out(ya, g, yb, x2, w, lng, lnb, *, seq, tm=512):
    m = x2.shape[0]
    tm = min(tm, seq)
    tps = seq // tm
    half = pl.BlockSpec((tm, HALF), lambda i: (i, 0))
    full = pl.BlockSpec((tm, D_MODEL), lambda i: (i, 0))
    gated = g is not None
    acts = [ya, g, yb] if gated else [ya, yb]
    act_specs = [pl.BlockSpec((tm, HALF), lambda i: (i % tps, i // tps)), half, half] if gated else [half, half]
    return pl.pallas_call(
        functools.partial(_mix_out_body, gated=gated),
        grid=(m // tm,),
        in_specs=act_specs + [full, _const_spec(w.shape), _const_spec(lng.shape), _const_spec(lnb.shape)],
        out_specs=full,
        out_shape=jax.ShapeDtypeStruct((m, D_MODEL), F32),
        compiler_params=_params("arbitrary"),
        name="mix_out",
    )(*acts, x2, w, lng, lnb)


def _shift_rows(g, s, tail):
    rolled = pltpu.roll(g, s, 0)
    head = rolled[:SUBLANES]
    row = lax.broadcasted_iota(jnp.int32, head.shape, 0)
    for r in range(s):
        head = jnp.where(row == r, tail[SUBLANES - s + r:SUBLANES - s + r + 1, :], head)
    return jnp.concatenate([head, rolled[SUBLANES:]], axis=0)


def _ffn_body(x_ref, wg_ref, wu_ref, cw_ref, cb_ref, wd_ref, lng_ref, lnb_ref, o_ref,
              xb_ref, acc_ref, act_ref, carry_ref, *, tiles_per_seq, chunk):
    i = pl.program_id(0)
    f = pl.program_id(1)
    nf = pl.num_programs(1)
    tm, tf = act_ref.shape

    @pl.when(f == 0)
    def _():
        xb_ref[...] = x_ref[...].astype(BF16)

    xb = xb_ref[...]
    first = (i % tiles_per_seq) == 0
    for c0 in range(0, tf, chunk):
        cs = slice(c0, min(c0 + chunk, tf))
        gate = _dot(xb, wg_ref[:, cs])
        up = _dot(xb, wu_ref[:, cs])
        tail = jnp.where(first, 0.0, carry_ref[f, :, cs])
        carry_ref[f, :, cs] = gate[tm - SUBLANES:tm, :]
        conv = cb_ref[:, cs] + cw_ref[FFN_CONV - 1:FFN_CONV, cs] * gate
        for s in range(1, FFN_CONV):
            conv = conv + cw_ref[FFN_CONV - 1 - s:FFN_CONV - s, cs] * _shift_rows(gate, s, tail)
        act_ref[:, cs] = (_silu(conv) * up).astype(BF16)
    down = _dot(act_ref[...], wd_ref[...])

    @pl.when(f == 0)
    def _():
        acc_ref[...] = down

    @pl.when(f == nf - 1)
    def _():
        o_ref[...] = _layer_norm(ALPHA * x_ref[...] + acc_ref[...] + down, lng_ref[...], lnb_ref[...])


def _ffn(x2, wg, wu, cw, cb, wd, lng, lnb, *, seq, tm=1024, tf=D_FF // 2, chunk=256):
    m = x2.shape[0]
    tm = min(tm, seq)
    nf = D_FF // tf
    assert nf == 2, "the accumulator is written on the first d_ff tile and consumed on the last"
    return pl.pallas_call(
        functools.partial(_ffn_body, tiles_per_seq=seq // tm, chunk=chunk),
        grid=(m // tm, nf),
        in_specs=[
            pl.BlockSpec((tm, D_MODEL), lambda i, f: (i, 0)),
            pl.BlockSpec((D_MODEL, tf), lambda i, f: (0, f)),
            pl.BlockSpec((D_MODEL, tf), lambda i, f: (0, f)),
            pl.BlockSpec((FFN_CONV, tf), lambda i, f: (0, f)),
            pl.BlockSpec((1, tf), lambda i, f: (0, f)),
            pl.BlockSpec((tf, D_MODEL), lambda i, f: (f, 0)),
            _const_spec(lng.shape), _const_spec(lnb.shape),
        ],
        out_specs=pl.BlockSpec((tm, D_MODEL), lambda i, f: (i, 0)),
        out_shape=jax.ShapeDtypeStruct((m, D_MODEL), F32),
        scratch_shapes=[pltpu.VMEM((tm, D_MODEL), BF16), pltpu.VMEM((tm, D_MODEL), F32),
                        pltpu.VMEM((tm, tf), BF16), pltpu.VMEM((nf, SUBLANES, tf), F32)],
        compiler_params=_params("arbitrary", "arbitrary"),
        name="conv_ffn",
    )(x2, wg, wu, cw, cb, wd, lng, lnb)


SB_SCALE = HEAD_DIM ** -0.5
assert math.frexp(SB_SCALE)[0] == 0.5, "folded into q before the bf16 cast, so it must be a power of two"
PAIR = 2 * LANES
N_PAIRS = N_HEADS // 2
L1_SB = 3 * HALF
L1_COLS_PAD = L1_SB + MLA_Q_LORA + MLA_KV_LORA + 2 * LANES


def _store_head_values(v_ref, v, even_head):
    v_ref[:, :HALF] = jnp.where(even_head, v, 0.0).astype(BF16)
    v_ref[:, HALF:] = jnp.where(even_head, 0.0, v).astype(BF16)


def _l1_in_body(x_ref, pos_ref, w_ref, invf_ref, qg_ref, wuq_ref, kvg_ref, wukv_ref,
                qsb_ref, ksb_ref, vsb_ref, q_ref, k_ref, v_ref):
    p = _dot(x_ref[...].astype(BF16), w_ref[...])
    qsb_ref[...] = (p[:, 0:HALF] * SB_SCALE).astype(BF16)
    ksb_ref[...] = p[:, HALF:2 * HALF].astype(BF16)
    lane = lax.broadcasted_iota(jnp.int32, (p.shape[0], HALF), 1)
    even_head = ((lane // HEAD_DIM) & 1) == 0
    _store_head_values(vsb_ref, p[:, 2 * HALF:3 * HALF], even_head)
    c_q = p[:, L1_SB:L1_SB + MLA_Q_LORA]
    c_kv = p[:, L1_SB + MLA_Q_LORA:L1_SB + MLA_Q_LORA + MLA_KV_LORA]
    off = L1_SB + MLA_Q_LORA + MLA_KV_LORA
    kpe = p[:, off:off + LANES]
    kpe_rot = p[:, off + LANES:off + 2 * LANES]
    ang = pos_ref[...].astype(F32) * invf_ref[...]
    cos = jnp.cos(ang)
    sin = jnp.sin(ang)
    kpe = (kpe * cos + kpe_rot * sin).astype(BF16)
    cqn = c_q * lax.rsqrt(jnp.mean(c_q * c_q, axis=-1, keepdims=True) + 1e-6) * qg_ref[...]
    q = _dot(cqn.astype(BF16), wuq_ref[...])
    ckn = c_kv * lax.rsqrt(jnp.mean(c_kv * c_kv, axis=-1, keepdims=True) + 1e-6) * kvg_ref[...]
    kv = _dot(ckn.astype(BF16), wukv_ref[...])
    for pr in range(N_PAIRS):
        q_ref[:, pr * PAIR:pr * PAIR + LANES] = q[:, pr * PAIR:pr * PAIR + LANES].astype(BF16)
        q_pe = q[:, pr * PAIR + LANES:(pr + 1) * PAIR]
        q_rot = q[:, N_PAIRS * PAIR + pr * LANES:N_PAIRS * PAIR + (pr + 1) * LANES]
        q_ref[:, pr * PAIR + LANES:(pr + 1) * PAIR] = (q_pe * cos + q_rot * sin).astype(BF16)
        k_ref[:, pr * PAIR:pr * PAIR + LANES] = kv[:, pr * LANES:(pr + 1) * LANES].astype(BF16)
        k_ref[:, pr * PAIR + LANES:(pr + 1) * PAIR] = kpe
    vt = kv[:, HALF:].T
    chan = lax.broadcasted_iota(jnp.int32, vt.shape, 0)
    even_chan = ((chan // MLA_V) & 1) == 0
    v_ref[:HALF, :] = jnp.where(even_chan, vt, 0.0).astype(BF16)
    v_ref[HALF:, :] = jnp.where(even_chan, 0.0, vt).astype(BF16)


def _l1_in(x2, pos2, w, invf, qg, wuq, kvg, wukv, *, seq, tm=512):
    m = x2.shape[0]
    tm = min(tm, seq)
    tps = seq // tm
    consts = [w, invf, qg, wuq, kvg, wukv]
    v_t = pl.BlockSpec((2 * HALF, tm), lambda i: (i // tps, i % tps))
    half = pl.BlockSpec((tm, HALF), lambda i: (i, 0))
    cat = pl.BlockSpec((tm, N_PAIRS * PAIR), lambda i: (i, 0))
    assert N_PAIRS * PAIR == 2 * HALF
    sd = jax.ShapeDtypeStruct
    return pl.pallas_call(
        _l1_in_body,
        grid=(m // tm,),
        in_specs=[pl.BlockSpec((tm, D_MODEL), lambda i: (i, 0)), pl.BlockSpec((tm, 1), lambda i: (i, 0))]
        + [_const_spec(c.shape) for c in consts],
        out_specs=[half, half, cat, cat, cat, v_t],
        out_shape=[sd((m, HALF), BF16)] * 2 + [sd((m, 2 * HALF), BF16)] * 3 + [sd((m // seq * 2 * HALF, seq), BF16)],
        compiler_params=_params("arbitrary"),
        name="l1_in",
    )(x2, pos2, *consts)


ATT_BLOCK = 256


def _stack_heads(q_ref, qs_ref, masks, width):
    tq = ATT_BLOCK
    for p in range(N_PAIRS):
        qp = q_ref[:, p * width:(p + 1) * width]
        for hh in range(2):
            qs_ref[p, hh * tq:(hh + 1) * tq, :] = jnp.where(masks[hh], qp, jnp.zeros_like(qp))


def _head_values(v_ref, start, p):
    return [v_ref[pl.ds(start, ATT_BLOCK), hh * HALF + p * LANES:hh * HALF + (p + 1) * LANES] for hh in range(2)]


def _sb_body(q_ref, k_ref, v_ref, o_ref, qs_ref, acc_ref, s_ref, w_ref):
    qi = pl.program_id(1)
    tq = ATT_BLOCK
    lane = lax.broadcasted_iota(jnp.int32, (tq, LANES), 1)
    _stack_heads(q_ref, qs_ref, [(lane // HEAD_DIM) == hh for hh in range(2)], LANES)
    acc_ref[...] = jnp.zeros_like(acc_ref)
    row = lax.broadcasted_iota(jnp.int32, (2 * tq, tq), 0) & (tq - 1)
    col = lax.broadcasted_iota(jnp.int32, (2 * tq, tq), 1)
    strict = col < row
    later = (lax.broadcasted_iota(jnp.int32, (tq, tq), 0) > lax.broadcasted_iota(jnp.int32, (tq, tq), 1)).astype(BF16)

    def scores(kj, slot):
        start = pl.multiple_of(kj * tq, tq)
        for p in range(N_PAIRS):
            s_ref[slot, p] = _dot_nt(qs_ref[p], k_ref[pl.ds(start, tq), p * LANES:(p + 1) * LANES])

    def weights(slot, runs, diagonal):
        new_runs = []
        for p in range(N_PAIRS):
            z = s_ref[slot, p]
            log_beta = jnp.minimum(z, 0.0) - jnp.log(1.0 + jnp.exp(-jnp.abs(z)))
            log_keep = log_beta - z
            if diagonal:
                log_keep = jnp.where(strict, log_keep, 0.0)
            suffix = _dot(log_keep.astype(BF16), later)
            att = jnp.exp(log_beta + suffix + runs[p])
            if diagonal:
                att = jnp.where(strict, att, 0.0)
            w_ref[slot, p] = att.astype(BF16)
            new_runs.append(runs[p] + jnp.sum(log_keep, axis=1, keepdims=True))
        return tuple(new_runs)

    def accumulate(kj, slot):
        start = pl.multiple_of(kj * tq, tq)
        for p in range(N_PAIRS):
            vh = _head_values(v_ref, start, p)
            w = w_ref[slot, p]
            acc_ref[p] += jnp.concatenate([_dot(w[:tq], vh[0]), _dot(w[tq:], vh[1])], axis=0)

    scores(qi, 0)
    scores(jnp.maximum(qi - 1, 0), 1)
    runs = weights(0, tuple(jnp.zeros((2 * tq, 1), F32) for _ in range(N_PAIRS)), True)

    def trip(i, runs):
        slot = i & 1
        accumulate(qi - i + 1, 1 - slot)
        runs = weights(slot, runs, False)
        scores(jnp.maximum(qi - i - 1, 0), 1 - slot)
        return runs

    lax.fori_loop(1, qi + 1, trip, runs)
    accumulate(0, qi & 1)
    for p in range(N_PAIRS):
        o_ref[:, p * LANES:(p + 1) * LANES] = (acc_ref[p, :tq] + acc_ref[p, tq:]).astype(o_ref.dtype)


def _sb_attention(q, k, v, *, batch, seq):
    tq = ATT_BLOCK
    nq = seq // tq
    return pl.pallas_call(
        _sb_body,
        grid=(batch, nq),
        in_specs=[
            pl.BlockSpec((tq, HALF), lambda b, i: (b * nq + i, 0)),
            pl.BlockSpec((seq, HALF), lambda b, i: (b, 0)),
            pl.BlockSpec((seq, 2 * HALF), lambda b, i: (b, 0)),
        ],
        out_specs=pl.BlockSpec((tq, HALF), lambda b, i: (b * nq + i, 0)),
        out_shape=jax.ShapeDtypeStruct((batch * seq, HALF), BF16),
        scratch_shapes=[pltpu.VMEM((N_PAIRS, 2 * tq, LANES), BF16), pltpu.VMEM((N_PAIRS, 2 * tq, LANES), F32),
                        pltpu.VMEM((2, N_PAIRS, 2 * tq, tq), F32), pltpu.VMEM((2, N_PAIRS, 2 * tq, tq), BF16)],
        compiler_params=_params("arbitrary", "arbitrary"),
        name="sb_attention",
    )(q, k, v)


def _mla_body(q_ref, k_ref, vt_ref, o_ref, qs_ref, acc_ref, s_ref, w_ref):
    qi = pl.program_id(1)
    tq = ATT_BLOCK
    scale = (MLA_NOPE + MLA_ROPE) ** -0.5
    lane2 = lax.broadcasted_iota(jnp.int32, (tq, PAIR), 1)
    masks = [((lane2 >= hh * MLA_NOPE) & (lane2 < (hh + 1) * MLA_NOPE))
             | ((lane2 >= LANES + hh * MLA_ROPE) & (lane2 < LANES + (hh + 1) * MLA_ROPE)) for hh in range(2)]
    _stack_heads(q_ref, qs_ref, masks, PAIR)
    acc_ref[...] = jnp.zeros_like(acc_ref)
    key = lax.broadcasted_iota(jnp.int32, (tq, 2 * tq), 0)
    qry = lax.broadcasted_iota(jnp.int32, (tq, 2 * tq), 1) & (tq - 1)
    causal = key <= qry

    def scores(kj, slot):
        start = pl.multiple_of(kj * tq, tq)
        for p in range(N_PAIRS):
            k = k_ref[pl.ds(start, tq), p * PAIR:(p + 1) * PAIR]
            s_ref[slot, p] = _dot_nt(k, qs_ref[p]) * scale

    def values(kj, slot, p):
        start = pl.multiple_of(kj * tq, tq)
        vt = [vt_ref[hh * HALF + p * LANES:hh * HALF + (p + 1) * LANES, pl.ds(start, tq)] for hh in range(2)]
        w = w_ref[slot, p]
        return jnp.concatenate([_dot(vt[0], w[:, :tq]), _dot(vt[1], w[:, tq:])], axis=1)

    def weights(slot, carry, diagonal, prev=None):
        pvs = [values(prev[0], prev[1], p) for p in range(N_PAIRS)] if prev is not None else None
        new = []
        for p in range(N_PAIRS):
            m, l = carry[p]
            s = s_ref[slot, p]
            if diagonal:
                s = jnp.where(causal, s, NEG_BIG)
            m_new = jnp.maximum(m, jnp.max(s, axis=0, keepdims=True))
            corr = jnp.exp(m - m_new)
            pexp = jnp.exp(s - m_new)
            l = l * corr + jnp.sum(pexp, axis=0, keepdims=True)
            w_ref[slot, p] = pexp.astype(BF16)
            if prev is not None:
                acc_ref[p] = (acc_ref[p] + pvs[p]) * corr
            new.append((m_new, l))
        return tuple(new)

    init = tuple((jnp.full((1, 2 * tq), NEG_BIG, F32), jnp.zeros((1, 2 * tq), F32)) for _ in range(N_PAIRS))
    scores(qi, 0)
    scores(0, 1)
    carry = weights(0, init, True)

    def trip(i, carry):
        slot = i & 1
        carry = weights(slot, carry, False, prev=(jnp.where(i == 1, qi, i - 2), 1 - slot))
        scores(jnp.minimum(i, jnp.maximum(qi - 1, 0)), 1 - slot)
        return carry

    carry = lax.fori_loop(1, qi + 1, trip, carry)
    last = (jnp.where(qi == 0, 0, qi - 1), qi & 1)
    for p in range(N_PAIRS):
        acc_ref[p] += values(last[0], last[1], p)
    for p in range(N_PAIRS):
        out = acc_ref[p] / carry[p][1]
        o_ref[:, p * LANES:(p + 1) * LANES] = (out[:, :tq] + out[:, tq:]).T.astype(o_ref.dtype)


def _mla_attention(q, k, v, *, batch, seq):
    tq = ATT_BLOCK
    nq = seq // tq
    return pl.pallas_call(
        _mla_body,
        grid=(batch, nq),
        in_specs=[
            pl.BlockSpec((tq, N_PAIRS * PAIR), lambda b, i: (b * nq + i, 0)),
            pl.BlockSpec((seq, N_PAIRS * PAIR), lambda b, i: (b, 0)),
            pl.BlockSpec((2 * HALF, seq), lambda b, i: (b, 0)),
        ],
        out_specs=pl.BlockSpec((tq, HALF), lambda b, i: (b * nq + i, 0)),
        out_shape=jax.ShapeDtypeStruct((batch * seq, HALF), BF16),
        scratch_shapes=[pltpu.VMEM((N_PAIRS, 2 * tq, PAIR), BF16), pltpu.VMEM((N_PAIRS, LANES, 2 * tq), F32),
                        pltpu.VMEM((2, N_PAIRS, tq, 2 * tq), F32), pltpu.VMEM((2, N_PAIRS, tq, 2 * tq), BF16)],
        compiler_params=_params("arbitrary", "arbitrary"),
        name="mla_attention",
    )(q, k, v)


def _row(v):
    return v.reshape(1, -1).astype(F32)


def _chain_layout(v):
    lane = jnp.arange(LANES)
    head = (2 * (lane % (LANES // 2)) + lane // (LANES // 2)) % N_HEADS
    return v.reshape(N_HEADS, HEAD_DIM).T[:, head].astype(F32)


def _rope_rotate_cols(w):
    half = MLA_ROPE // 2
    return jnp.concatenate([-w[..., half:], w[..., :half]], axis=-1)


def _layer0_mixer(x2, batch, seq, l0_w_in, rwkv_mix, rwkv_w0, rwkv_w2, rwkv_a0, rwkv_a2, rwkv_g2,
                  rwkv_k_k, rwkv_k_a, rwkv_r_k, rwkv_ln_g, rwkv_ln_b, ssm_conv_w, ssm_conv_b,
                  ssm_dt_bias, ssm_a_log, ssm_d, ssm_norm_g):
    w_r = l0_w_in[:, :RWKV_COLS].astype(BF16)
    zeros = jnp.zeros((RWKV_DECAY_LORA, HALF), F32)
    wl = jnp.concatenate([jnp.concatenate([rwkv_w2, zeros], axis=1),
                          jnp.concatenate([zeros, rwkv_a2], axis=1)], axis=0).astype(BF16)
    w0a0 = _row(jnp.concatenate([rwkv_w0, rwkv_a0]))
    *rkvwa, gate = _rwkv_in(x2, w_r, _row(rwkv_mix), wl, w0a0, rwkv_g2.astype(BF16), seq=seq)
    as_tiles = lambda t: t.reshape(seq, HEAD_DIM, LANES)
    y = _rwkv_scan(*[as_tiles(t) for t in rkvwa], _chain_layout(rwkv_k_k), _chain_layout(rwkv_k_a),
                   _chain_layout(rwkv_r_k.reshape(-1)), _chain_layout(rwkv_ln_g), _chain_layout(rwkv_ln_b))
    y_a = y.reshape(seq, batch * HALF)

    w_s = l0_w_in[:, RWKV_COLS:]
    pad = jnp.zeros((D_MODEL, LANES - N_HEADS), F32)
    w_s = jnp.concatenate([w_s[:, HALF:HALF + SSM_XBC], w_s[:, :HALF], w_s[:, HALF + SSM_XBC:], pad], axis=1)
    ps = _proj(x2, w_s.astype(BF16))
    dt_raw = ps[:, SSM_XBC + HALF:SSM_XBC + HALF + N_HEADS]
    dtt = dt_raw.reshape(batch, seq, N_HEADS).transpose(0, 2, 1)
    a_neg = -jnp.exp(ssm_a_log.astype(F32))
    lane_pad = jnp.zeros((LANES - N_HEADS,), F32)
    expand = jnp.concatenate([jnp.repeat(jnp.eye(N_HEADS, dtype=F32), HEAD_DIM, axis=1),
                              jnp.zeros((LANES - N_HEADS, HALF), F32)], axis=0)
    y_b = _ssd(ps, dtt, ssm_conv_w.astype(F32), _row(ssm_conv_b),
               _row(jnp.concatenate([ssm_dt_bias, lane_pad])), ssm_dt_bias.reshape(N_HEADS, 1).astype(F32),
               _row(jnp.concatenate([a_neg, lane_pad])), a_neg.reshape(N_HEADS, 1),
               expand, _row(jnp.repeat(ssm_d, HEAD_DIM)), _row(ssm_norm_g), batch=batch, seq=seq)
    return y_a, gate, y_b


def _layer1_mixer(h2, positions, batch, seq, l1_w_in, mla_q_norm_g, mla_w_uq, mla_kv_norm_g, mla_w_ukv):
    off = L1_SB + MLA_Q_LORA + MLA_KV_LORA
    w_kpe = l1_w_in[:, off:off + MLA_ROPE]
    lane_zeros = jnp.zeros((D_MODEL, LANES - 2 * MLA_ROPE), F32)
    w_kpe_blk = jnp.concatenate([w_kpe, w_kpe, lane_zeros], axis=1)
    w_kpe_rot = _rope_rotate_cols(w_kpe)
    w_kpe_rot_blk = jnp.concatenate([w_kpe_rot, w_kpe_rot, lane_zeros], axis=1)
    w_in = jnp.concatenate([l1_w_in[:, :off], w_kpe_blk, w_kpe_rot_blk], axis=1).astype(BF16)

    inv_freq = 1.0 / (ROPE_THETA ** (jnp.arange(0, MLA_ROPE, 2, dtype=F32) / MLA_ROPE))
    invf = jnp.concatenate([jnp.tile(inv_freq, 4), jnp.zeros((LANES - 2 * MLA_ROPE,), F32)]).reshape(1, LANES)

    wq = mla_w_uq.reshape(MLA_Q_LORA, N_HEADS, MLA_NOPE + MLA_ROPE)
    wq_nope = wq[:, :, :MLA_NOPE].reshape(MLA_Q_LORA, N_PAIRS, 2 * MLA_NOPE)
    wq_pe = wq[:, :, MLA_NOPE:]
    lz = jnp.zeros((MLA_Q_LORA, N_PAIRS, LANES - 2 * MLA_ROPE), F32)
    wq_pe_blk = jnp.concatenate([wq_pe.reshape(MLA_Q_LORA, N_PAIRS, 2 * MLA_ROPE), lz], axis=2)
    wq_rot_blk = jnp.concatenate([_rope_rotate_cols(wq_pe).reshape(MLA_Q_LORA, N_PAIRS, 2 * MLA_ROPE), lz], axis=2)
    wuq = jnp.concatenate([jnp.concatenate([wq_nope, wq_pe_blk], axis=2).reshape(MLA_Q_LORA, N_PAIRS * PAIR),
                           wq_rot_blk.reshape(MLA_Q_LORA, N_PAIRS * LANES)], axis=1).astype(BF16)

    wkv = mla_w_ukv.reshape(MLA_KV_LORA, N_HEADS, MLA_NOPE + MLA_V)
    wukv = jnp.concatenate([wkv[:, :, :MLA_NOPE].reshape(MLA_KV_LORA, HALF),
                            wkv[:, :, MLA_NOPE:].reshape(MLA_KV_LORA, HALF)], axis=1).astype(BF16)

    qsb, ksb, vsb, q, k, v = _l1_in(h2, positions.reshape(batch * seq, 1), w_in, invf,
                                    _row(mla_q_norm_g), wuq, _row(mla_kv_norm_g), wukv, seq=seq)
    y_c = _sb_attention(qsb, ksb, vsb, batch=batch, seq=seq)
    y_d = _mla_attention(q, k, v, batch=batch, seq=seq)
    return y_c, y_d


def _ffn_layer(h2, seq, w_up, conv_w, conv_b, w_down, ln_g, ln_b):
    return _ffn(h2, w_up[:, :D_FF].astype(BF16), w_up[:, D_FF:].astype(BF16), conv_w.astype(F32),
                _row(conv_b), w_down.astype(BF16), _row(ln_g), _row(ln_b), seq=seq)


def kernel(x, positions, l0_w_in, rwkv_mix, rwkv_w0, rwkv_w2, rwkv_a0, rwkv_a2, rwkv_g2, rwkv_k_k, rwkv_k_a, rwkv_r_k, rwkv_ln_g, rwkv_ln_b, ssm_conv_w, ssm_conv_b, ssm_dt_bias, ssm_a_log, ssm_d, ssm_norm_g, l0_w_out, l0_ln1_g, l0_ln1_b, ffn0_w_up, ffn0_conv_w, ffn0_conv_b, ffn0_w_down, l0_ln2_g, l0_ln2_b, l1_w_in, mla_q_norm_g, mla_w_uq, mla_kv_norm_g, mla_w_ukv, l1_w_out, l1_ln1_g, l1_ln1_b, ffn1_w_up, ffn1_conv_w, ffn1_conv_b, ffn1_w_down, l1_ln2_g, l1_ln2_b):
    batch, seq, _ = x.shape
    assert batch * N_HEADS == LANES, "the RWKV scan maps batch*heads onto the lane axis"
    x2 = x.reshape(batch * seq, D_MODEL).astype(F32)

    y_a, gate, y_b = _layer0_mixer(x2, batch, seq, l0_w_in, rwkv_mix, rwkv_w0, rwkv_w2, rwkv_a0, rwkv_a2,
                                   rwkv_g2, rwkv_k_k, rwkv_k_a, rwkv_r_k, rwkv_ln_g, rwkv_ln_b, ssm_conv_w,
                                   ssm_conv_b, ssm_dt_bias, ssm_a_log, ssm_d, ssm_norm_g)
    h = _mix_out(y_a, gate, y_b, x2, l0_w_out.astype(BF16), _row(l0_ln1_g), _row(l0_ln1_b), seq=seq)
    h = _ffn_layer(h, seq, ffn0_w_up, ffn0_conv_w, ffn0_conv_b, ffn0_w_down, l0_ln2_g, l0_ln2_b)

    y_c, y_d = _layer1_mixer(h, positions, batch, seq, l1_w_in, mla_q_norm_g, mla_w_uq, mla_kv_norm_g, mla_w_ukv)
    h = _mix_out(y_c, None, y_d, h, l1_w_out.astype(BF16), _row(l1_ln1_g), _row(l1_ln1_b), seq=seq)
    h = _ffn_layer(h, seq, ffn1_w_up, ffn1_conv_w, ffn1_conv_b, ffn1_w_down, l1_ln2_g, l1_ln2_b)
    return h.reshape(batch, seq, D_MODEL).astype(x.dtype)
```

```python
import functools
import math

import jax
import jax.numpy as jnp
from jax import lax
from jax.experimental import pallas as pl
from jax.experimental.pallas import tpu as pltpu

F32 = jnp.float32
BF16 = jnp.bfloat16

D_MODEL = 1024
HEAD_DIM = 64
N_HEADS = 8
HALF = N_HEADS * HEAD_DIM
RWKV_DECAY_LORA = 64
RWKV_A_LORA = 64
RWKV_GATE_LORA = 128
RWKV_GN_EPS = 64e-5
RWKV_COLS = 3 * HALF + RWKV_DECAY_LORA + RWKV_A_LORA + RWKV_GATE_LORA
SSM_GROUPS = 2
SSM_STATE = 128
SSM_CONV = 4
SSM_CHUNK = 128
SSM_XBC = HALF + 2 * SSM_GROUPS * SSM_STATE
SSM_COLS = HALF + SSM_XBC + N_HEADS
SSM_COLS_PAD = SSM_XBC + HALF + 128
MLA_NOPE = 64
MLA_ROPE = 32
MLA_V = 64
MLA_Q_LORA = 256
MLA_KV_LORA = 128
ROPE_THETA = 10000.0
D_FF = 2816
FFN_CONV = 3
DEPTH = 2
ALPHA = (2 * DEPTH) ** 0.25
LN_EPS = 1e-5

LANES = 128
SUBLANES = 8
VMEM_LIMIT = 56 * 1024 * 1024
NEG_BIG = -1e30

_HI = lax.Precision.HIGHEST


def _params(*sem):
    return pltpu.CompilerParams(dimension_semantics=sem, vmem_limit_bytes=VMEM_LIMIT)


def _dot(a, b, precision=None):
    return jnp.dot(a, b, preferred_element_type=F32, precision=precision)


def _dot_nt(a, b):
    return lax.dot_general(a, b, (((1,), (1,)), ((), ())), preferred_element_type=F32)


def _sigmoid(x):
    return 1.0 / (1.0 + jnp.exp(-x))


def _softplus(x):
    return jnp.maximum(x, 0.0) + jnp.log1p(jnp.exp(-jnp.abs(x)))


def _silu(x):
    return x * _sigmoid(x)


def _layer_norm(v, g, b):
    mu = jnp.mean(v, axis=-1, keepdims=True)
    d = v - mu
    var = jnp.mean(d * d, axis=-1, keepdims=True)
    return d * lax.rsqrt(var + LN_EPS) * g + b


def _const_spec(shape):
    nd = len(shape)
    return pl.BlockSpec(shape, lambda *_: (0,) * nd)


def _rwkv_in_body(x_ref, w_ref, mix_ref, wl_ref, w0a0_ref, g2_ref, ws_ref, r_ref, k_ref, v_ref, dec_ref, a_ref, g_ref,
                  ps_ref, carry_ref, *, tiles_per_seq):
    i = pl.program_id(0)
    xb = x_ref[...].astype(BF16)
    ps_ref[...] = _dot(xb, ws_ref[...])
    p = _dot(xb, w_ref[...])
    tm = p.shape[0]
    first = (i % tiles_per_seq) == 0
    prev_row = jnp.where(first, 0.0, carry_ref[SUBLANES - 1:SUBLANES, :])
    row = lax.broadcasted_iota(jnp.int32, p.shape, 0)
    prev = jnp.where(row == 0, prev_row, pltpu.roll(p, 1, 0))
    carry_ref[...] = p[tm - SUBLANES:tm, :]
    pm = p + (prev - p) * mix_ref[...]
    lo = pm[:, 3 * HALF:3 * HALF + LANES]
    lane = lax.broadcasted_iota(jnp.int32, lo.shape, 1)
    lo = jnp.where(lane < RWKV_DECAY_LORA, jnp.tanh(lo), lo)
    wa = _dot(lo.astype(BF16), wl_ref[...]) + w0a0_ref[...]
    log_w = -_softplus(-wa[:, :HALF]) - 0.5
    r_ref[...] = pm[:, 0:HALF]
    k_ref[...] = pm[:, HALF:2 * HALF]
    v_ref[...] = pm[:, 2 * HALF:3 * HALF]
    dec_ref[...] = jnp.exp(-jnp.exp(log_w))
    a_ref[...] = _sigmoid(wa[:, HALF:])
    g_lo = _sigmoid(pm[:, 3 * HALF + LANES:])
    g_ref[...] = _dot(g_lo.astype(BF16), g2_ref[...])


def _rwkv_in(x2, w_r, mix, wl, w0a0, g2, w_s, *, seq, tm=512):
    m = x2.shape[0]
    tm = min(tm, seq)
    tps = seq // tm
    batch = m // seq
    body = functools.partial(_rwkv_in_body, tiles_per_seq=tps)
    time_major = pl.BlockSpec((tm, HALF), lambda i: (i % tps, i // tps))
    return pl.pallas_call(
        body,
        grid=(m // tm,),
        in_specs=[
            pl.BlockSpec((tm, D_MODEL), lambda i: (i, 0)),
            _const_spec(w_r.shape), _const_spec(mix.shape), _const_spec(wl.shape),
            _const_spec(w0a0.shape), _const_spec(g2.shape), _const_spec(w_s.shape),
        ],
        out_specs=[time_major] * 5 + [pl.BlockSpec((tm, HALF), lambda i: (i, 0)),
                                      pl.BlockSpec((tm, w_s.shape[1]), lambda i: (i, 0))],
        out_shape=[jax.ShapeDtypeStruct((seq, batch * HALF), F32)] * 5
        + [jax.ShapeDtypeStruct((m, HALF), F32), jax.ShapeDtypeStruct((m, w_s.shape[1]), F32)],
        scratch_shapes=[pltpu.VMEM((SUBLANES, RWKV_COLS), F32)],
        compiler_params=_params("arbitrary"),
        name="rwkv_in",
    )(x2, w_r, mix, wl, w0a0, g2, w_s)


SCAN_UNROLL = 16


def _to_chain_lanes(z):
    zt = z.T
    return jnp.concatenate([zt[:HEAD_DIM], zt[HEAD_DIM:]], axis=1)


def _from_chain_lanes(y):
    return jnp.concatenate([y[:, :LANES // 2], y[:, LANES // 2:]], axis=0).T


def _rwkv_scan_body(r_ref, k_ref, v_ref, w_ref, a_ref, kk_ref, ka_ref, rk_ref, lng_ref, lnb_ref, o_ref,
                    h_ref, rs_ref, ws_ref, vs_ref, av_ref, bv_ref, k2_ref, bon_ref, ys_ref, *, tb):
    @pl.when(pl.program_id(0) == 0)
    def _():
        h_ref[...] = jnp.zeros_like(h_ref)

    for t in range(tb):
        r = _to_chain_lanes(r_ref[t])
        k = _to_chain_lanes(k_ref[t])
        v = _to_chain_lanes(v_ref[t])
        a = _to_chain_lanes(a_ref[t])
        kk = k * kk_ref[...]
        nrm = jnp.sqrt(jnp.sum(kk * kk, axis=0, keepdims=True))
        kk = kk / jnp.maximum(nrm, 1e-12)
        k2 = k * (1.0 + (a - 1.0) * ka_ref[...])
        rs_ref[t] = r
        vs_ref[t] = v
        ws_ref[t] = _to_chain_lanes(w_ref[t])
        av_ref[t] = -kk
        bv_ref[t] = kk * a
        k2_ref[t] = k2
        bon_ref[t] = jnp.sum(r * k2 * rk_ref[...], axis=0, keepdims=True) * v

    def step(t, carry):
        zero = jnp.zeros((HEAD_DIM, LANES), F32)

        def read(j, u):
            return u + h_ref[j] * av_ref[t, pl.ds(j, 1), :]

        u = lax.fori_loop(0, HEAD_DIM, read, zero, unroll=SCAN_UNROLL)
        vt = vs_ref[t]

        def update(j, y):
            wj = ws_ref[t, pl.ds(j, 1), :]
            rj = rs_ref[t, pl.ds(j, 1), :]
            bj = bv_ref[t, pl.ds(j, 1), :]
            kj = k2_ref[t, pl.ds(j, 1), :]
            hn = h_ref[j] * wj + u * bj + vt * kj
            h_ref[j] = hn
            return y + hn * rj

        ys_ref[t] = lax.fori_loop(0, HEAD_DIM, update, zero, unroll=SCAN_UNROLL)
        return carry

    lax.fori_loop(0, tb, step, 0)

    for t in range(tb):
        y = ys_ref[t]
        mu = jnp.mean(y, axis=0, keepdims=True)
        d = y - mu
        var = jnp.mean(d * d, axis=0, keepdims=True)
        y = d * lax.rsqrt(var + RWKV_GN_EPS) * lng_ref[...] + lnb_ref[...] + bon_ref[t]
        o_ref[t] = _from_chain_lanes(y)


def _rwkv_scan(r, k, v, w, a, kk, ka, rk, lng, lnb, *, tb=16):
    t = r.shape[0]
    vec = pl.BlockSpec((HEAD_DIM, LANES), lambda i: (0, 0))
    blk = (tb, HEAD_DIM, LANES)
    seq_spec = pl.BlockSpec(blk, lambda i: (i, 0, 0))
    return pl.pallas_call(
        functools.partial(_rwkv_scan_body, tb=tb),
        grid=(t // tb,),
        in_specs=[seq_spec] * 5 + [vec] * 5,
        out_specs=seq_spec,
        out_shape=jax.ShapeDtypeStruct((t, HEAD_DIM, LANES), F32),
        scratch_shapes=[pltpu.VMEM((HEAD_DIM, HEAD_DIM, LANES), F32)] + [pltpu.VMEM(blk, F32)] * 8,
        compiler_params=_params("arbitrary"),
        name="rwkv_scan",
    )(r, k, v, w, a, kk, ka, rk, lng, lnb)


def _ssd_body(xbc_ref, z_ref, dt_ref, dtt_ref, cw_ref, cb_ref, dtb_row_ref, dtb_col_ref,
              a_row_ref, a_col_ref, e_ref, dsk_ref, ng_ref, o_ref, ext_ref, st_ref):
    L = SSM_CHUNK

    @pl.when(pl.program_id(1) == 0)
    def _():
        ext_ref[...] = jnp.zeros_like(ext_ref)
        st_ref[...] = jnp.zeros_like(st_ref)

    xbc = xbc_ref[...]
    tail = ext_ref[...]
    ext_ref[...] = xbc[L - SUBLANES:L, :]
    conv = cb_ref[...] + cw_ref[SSM_CONV - 1:SSM_CONV, :] * xbc
    for s in range(1, SSM_CONV):
        conv = conv + cw_ref[SSM_CONV - 1 - s:SSM_CONV - s, :] * _shift_rows(xbc, s, tail)
    xc = _silu(conv)
    xs = xc[:, :HALF]
    bm = xc[:, HALF:HALF + SSM_GROUPS * SSM_STATE]
    cm = xc[:, HALF + SSM_GROUPS * SSM_STATE:]

    row = lax.broadcasted_iota(jnp.int32, (L, L), 0)
    col = lax.broadcasted_iota(jnp.int32, (L, L), 1)
    lower = row >= col
    dtc = _softplus(dt_ref[...] + dtb_row_ref[...])
    cum_col = _dot(lower.astype(F32), dtc * a_row_ref[...], _HI)
    dtr = _softplus(dtt_ref[0] + dtb_col_ref[...])
    cum_row = _dot(dtr * a_col_ref[...], (row <= col).astype(F32), _HI)
    dtx = _dot(dtc, e_ref[...], _HI)
    ccx = _dot(cum_col, e_ref[...], _HI)
    x = xs * dtx
    ecc = jnp.exp(ccx)
    clast = ccx[L - 1:L, :]
    xd = x * jnp.exp(clast - ccx)
    elast = jnp.exp(clast)
    lane = lax.broadcasted_iota(jnp.int32, (L, LANES), 1)

    ys = []
    for g in range(SSM_GROUPS):
        bg = bm[:, g * SSM_STATE:(g + 1) * SSM_STATE]
        cg = cm[:, g * SSM_STATE:(g + 1) * SSM_STATE].astype(BF16)
        cbm = _dot_nt(cg, bg.astype(BF16))
        bgt = bg.T.astype(BF16)
        for q in range(2):
            p = g * 2 + q
            sl = slice(p * LANES, (p + 1) * LANES)
            st = st_ref[p]
            yp = _dot(cg, st.astype(BF16)) * ecc[:, sl]
            xp = x[:, sl]
            for hh in range(2):
                h = 2 * p + hh
                seg = cum_col[:, h:h + 1] - cum_row[h:h + 1, :]
                dec = jnp.exp(jnp.where(lower, seg, NEG_BIG))
                xh = jnp.where((lane // HEAD_DIM) == hh, xp, 0.0).astype(BF16)
                yp = yp + _dot((cbm * dec).astype(BF16), xh)
            st_ref[p] = st * elast[:, sl] + _dot(bgt, xd[:, sl].astype(BF16))
            ys.append(yp)
    y = jnp.concatenate(ys, axis=1) + xs * dsk_ref[...]
    u = y * _silu(z_ref[...])
    gw = HALF // SSM_GROUPS
    outs = []
    for g in range(SSM_GROUPS):
        ug = u[:, g * gw:(g + 1) * gw]
        outs.append(ug * lax.rsqrt(jnp.mean(ug * ug, axis=-1, keepdims=True) + 1e-5))
    o_ref[...] = (jnp.concatenate(outs, axis=1) * ng_ref[...]).astype(o_ref.dtype)


def _ssd(ps, dtt, cw, cb, dtb_row, dtb_col, a_row, a_col, e, dsk, ng, *, batch, seq):
    L = SSM_CHUNK
    nc = seq // L
    consts = [cw, cb, dtb_row, dtb_col, a_row, a_col, e, dsk, ng]
    return pl.pallas_call(
        _ssd_body,
        grid=(batch, nc),
        in_specs=[
            pl.BlockSpec((L, SSM_XBC), lambda b, c: (b * nc + c, 0)),
            pl.BlockSpec((L, HALF), lambda b, c: (b * nc + c, SSM_XBC // HALF)),
            pl.BlockSpec((L, LANES), lambda b, c: (b * nc + c, (SSM_XBC + HALF) // LANES)),
            pl.BlockSpec((1, N_HEADS, L), lambda b, c: (b, 0, c)),
        ] + [_const_spec(c.shape) for c in consts],
        out_specs=pl.BlockSpec((L, HALF), lambda b, c: (b * nc + c, 0)),
        out_shape=jax.ShapeDtypeStruct((batch * seq, HALF), BF16),
        scratch_shapes=[pltpu.VMEM((SUBLANES, SSM_XBC), F32),
                        pltpu.VMEM((N_HEADS // 2, SSM_STATE, LANES), F32)],
        compiler_params=_params("arbitrary", "arbitrary"),
        name="ssd",
    )(ps, ps, ps, dtt, *consts)


def _mix_out_body(*refs, gated):
    if gated:
        ya_ref, g_ref, yb_ref, x_ref, w_ref, lng_ref, lnb_ref, o_ref = refs
        ya = ya_ref[...] * g_ref[...]
    else:
        ya_ref, yb_ref, x_ref, w_ref, lng_ref, lnb_ref, o_ref = refs
        ya = ya_ref[...]
    mixed = _dot(ya.astype(BF16), w_ref[0:HALF, :]) + _dot(yb_ref[...].astype(BF16), w_ref[HALF:, :])
    o_ref[...] = _layer_norm(ALPHA * x_ref[...] + mixed, lng_ref[...], lnb_ref[...])


def _mix_out(ya, g, yb, x2, w, lng, lnb, *, seq, tm=512):
    m = x2.shape[0]
    tm = min(tm, seq)
    tps = seq // tm
    half = pl.BlockSpec((tm, HALF), lambda i: (i, 0))
    full = pl.BlockSpec((tm, D_MODEL), lambda i: (i, 0))
    gated = g is not None
    acts = [ya, g, yb] if gated else [ya, yb]
    act_specs = [pl.BlockSpec((tm, HALF), lambda i: (i % tps, i // tps)), half, half] if gated else [half, half]
    return pl.pallas_call(
        functools.partial(_mix_out_body, gated=gated),
        grid=(m // tm,),
        in_specs=act_specs + [full, _const_spec(w.shape), _const_spec(lng.shape), _const_spec(lnb.shape)],
        out_specs=full,
        out_shape=jax.ShapeDtypeStruct((m, D_MODEL), F32),
        compiler_params=_params("arbitrary"),
        name="mix_out",
    )(*acts, x2, w, lng, lnb)


def _shift_rows(g, s, tail):
    rolled = pltpu.roll(g, s, 0)
    head = rolled[:SUBLANES]
    row = lax.broadcasted_iota(jnp.int32, head.shape, 0)
    for r in range(s):
        head = jnp.where(row == r, tail[SUBLANES - s + r:SUBLANES - s + r + 1, :], head)
    return jnp.concatenate([head, rolled[SUBLANES:]], axis=0)


def _ffn_body(x_ref, wg_ref, wu_ref, cw_ref, cb_ref, wd_ref, lng_ref, lnb_ref, o_ref,
              xb_ref, acc_ref, act_ref, carry_ref, *, tiles_per_seq, chunk):
    i = pl.program_id(0)
    f = pl.program_id(1)
    nf = pl.num_programs(1)
    tm, tf = act_ref.shape

    @pl.when(f == 0)
    def _():
        xb_ref[...] = x_ref[...].astype(BF16)

    xb = xb_ref[...]
    first = (i % tiles_per_seq) == 0
    for c0 in range(0, tf, chunk):
        cs = slice(c0, min(c0 + chunk, tf))
        gate = _dot(xb, wg_ref[:, cs])
        up = _dot(xb, wu_ref[:, cs])
        tail = jnp.where(first, 0.0, carry_ref[f, :, cs])
        carry_ref[f, :, cs] = gate[tm - SUBLANES:tm, :]
        conv = cb_ref[:, cs] + cw_ref[FFN_CONV - 1:FFN_CONV, cs] * gate
        for s in range(1, FFN_CONV):
            conv = conv + cw_ref[FFN_CONV - 1 - s:FFN_CONV - s, cs] * _shift_rows(gate, s, tail)
        act_ref[:, cs] = (_silu(conv) * up).astype(BF16)
    down = _dot(act_ref[...], wd_ref[...])

    @pl.when(f == 0)
    def _():
        acc_ref[...] = down

    @pl.when(f == nf - 1)
    def _():
        o_ref[...] = _layer_norm(ALPHA * x_ref[...] + acc_ref[...] + down, lng_ref[...], lnb_ref[...])


def _ffn(x2, wg, wu, cw, cb, wd, lng, lnb, *, seq, tm=1024, tf=D_FF // 2, chunk=256):
    m = x2.shape[0]
    tm = min(tm, seq)
    nf = D_FF // tf
    assert nf == 2, "the accumulator is written on the first d_ff tile and consumed on the last"
    return pl.pallas_call(
        functools.partial(_ffn_body, tiles_per_seq=seq // tm, chunk=chunk),
        grid=(m // tm, nf),
        in_specs=[
            pl.BlockSpec((tm, D_MODEL), lambda i, f: (i, 0)),
            pl.BlockSpec((D_MODEL, tf), lambda i, f: (0, f)),
            pl.BlockSpec((D_MODEL, tf), lambda i, f: (0, f)),
            pl.BlockSpec((FFN_CONV, tf), lambda i, f: (0, f)),
            pl.BlockSpec((1, tf), lambda i, f: (0, f)),
            pl.BlockSpec((tf, D_MODEL), lambda i, f: (f, 0)),
            _const_spec(lng.shape), _const_spec(lnb.shape),
        ],
        out_specs=pl.BlockSpec((tm, D_MODEL), lambda i, f: (i, 0)),
        out_shape=jax.ShapeDtypeStruct((m, D_MODEL), F32),
        scratch_shapes=[pltpu.VMEM((tm, D_MODEL), BF16), pltpu.VMEM((tm, D_MODEL), F32),
                        pltpu.VMEM((tm, tf), BF16), pltpu.VMEM((nf, SUBLANES, tf), F32)],
        compiler_params=_params("arbitrary", "arbitrary"),
        name="conv_ffn",
    )(x2, wg, wu, cw, cb, wd, lng, lnb)


SB_SCALE = HEAD_DIM ** -0.5
assert math.frexp(SB_SCALE)[0] == 0.5, "folded into q before the bf16 cast, so it must be a power of two"
PAIR = 2 * LANES
N_PAIRS = N_HEADS // 2
L1_SB = 3 * HALF
L1_COLS_PAD = L1_SB + MLA_Q_LORA + MLA_KV_LORA + 2 * LANES


def _store_head_values(v_ref, v, even_head):
    v_ref[:, :HALF] = jnp.where(even_head, v, 0.0).astype(BF16)
    v_ref[:, HALF:] = jnp.where(even_head, 0.0, v).astype(BF16)


def _l1_in_body(x_ref, pos_ref, w_ref, invf_ref, qg_ref, wuq_ref, kvg_ref, wukv_ref,
                qsb_ref, ksb_ref, vsb_ref, q_ref, k_ref, v_ref):
    p = _dot(x_ref[...].astype(BF16), w_ref[...])
    qsb_ref[...] = (p[:, 0:HALF] * SB_SCALE).astype(BF16)
    ksb_ref[...] = p[:, HALF:2 * HALF].astype(BF16)
    lane = lax.broadcasted_iota(jnp.int32, (p.shape[0], HALF), 1)
    even_head = ((lane // HEAD_DIM) & 1) == 0
    _store_head_values(vsb_ref, p[:, 2 * HALF:3 * HALF], even_head)
    c_q = p[:, L1_SB:L1_SB + MLA_Q_LORA]
    c_kv = p[:, L1_SB + MLA_Q_LORA:L1_SB + MLA_Q_LORA + MLA_KV_LORA]
    off = L1_SB + MLA_Q_LORA + MLA_KV_LORA
    kpe = p[:, off:off + LANES]
    kpe_rot = p[:, off + LANES:off + 2 * LANES]
    ang = pos_ref[...].astype(F32) * invf_ref[...]
    cos = jnp.cos(ang)
    sin = jnp.sin(ang)
    kpe = (kpe * cos + kpe_rot * sin).astype(BF16)
    cqn = c_q * lax.rsqrt(jnp.mean(c_q * c_q, axis=-1, keepdims=True) + 1e-6) * qg_ref[...]
    q = _dot(cqn.astype(BF16), wuq_ref[...])
    ckn = c_kv * lax.rsqrt(jnp.mean(c_kv * c_kv, axis=-1, keepdims=True) + 1e-6) * kvg_ref[...]
    kv = _dot(ckn.astype(BF16), wukv_ref[...])
    for pr in range(N_PAIRS):
        q_ref[:, pr * PAIR:pr * PAIR + LANES] = q[:, pr * PAIR:pr * PAIR + LANES].astype(BF16)
        q_pe = q[:, pr * PAIR + LANES:(pr + 1) * PAIR]
        q_rot = q[:, N_PAIRS * PAIR + pr * LANES:N_PAIRS * PAIR + (pr + 1) * LANES]
        q_ref[:, pr * PAIR + LANES:(pr + 1) * PAIR] = (q_pe * cos + q_rot * sin).astype(BF16)
        k_ref[:, pr * PAIR:pr * PAIR + LANES] = kv[:, pr * LANES:(pr + 1) * LANES].astype(BF16)
        k_ref[:, pr * PAIR + LANES:(pr + 1) * PAIR] = kpe
    vt = kv[:, HALF:].T
    chan = lax.broadcasted_iota(jnp.int32, vt.shape, 0)
    even_chan = ((chan // MLA_V) & 1) == 0
    v_ref[:HALF, :] = jnp.where(even_chan, vt, 0.0).astype(BF16)
    v_ref[HALF:, :] = jnp.where(even_chan, 0.0, vt).astype(BF16)


def _l1_in(x2, pos2, w, invf, qg, wuq, kvg, wukv, *, seq, tm=512):
    m = x2.shape[0]
    tm = min(tm, seq)
    tps = seq // tm
    consts = [w, invf, qg, wuq, kvg, wukv]
    v_t = pl.BlockSpec((2 * HALF, tm), lambda i: (i // tps, i % tps))
    half = pl.BlockSpec((tm, HALF), lambda i: (i, 0))
    cat = pl.BlockSpec((tm, N_PAIRS * PAIR), lambda i: (i, 0))
    assert N_PAIRS * PAIR == 2 * HALF
    sd = jax.ShapeDtypeStruct
    return pl.pallas_call(
        _l1_in_body,
        grid=(m // tm,),
        in_specs=[pl.BlockSpec((tm, D_MODEL), lambda i: (i, 0)), pl.BlockSpec((tm, 1), lambda i: (i, 0))]
        + [_const_spec(c.shape) for c in consts],
        out_specs=[half, half, cat, cat, cat, v_t],
        out_shape=[sd((m, HALF), BF16)] * 2 + [sd((m, 2 * HALF), BF16)] * 3 + [sd((m // seq * 2 * HALF, seq), BF16)],
        compiler_params=_params("arbitrary"),
        name="l1_in",
    )(x2, pos2, *consts)


ATT_BLOCK = 256


def _stack_heads(q_ref, qs_ref, masks, width):
    tq = ATT_BLOCK
    for p in range(N_PAIRS):
        qp = q_ref[:, p * width:(p + 1) * width]
        for hh in range(2):
            qs_ref[p, hh * tq:(hh + 1) * tq, :] = jnp.where(masks[hh], qp, jnp.zeros_like(qp))


def _head_values(v_ref, start, p):
    return [v_ref[pl.ds(start, ATT_BLOCK), hh * HALF + p * LANES:hh * HALF + (p + 1) * LANES] for hh in range(2)]


def _sb_body(q_ref, k_ref, v_ref, o_ref, qs_ref, acc_ref, s_ref, w_ref):
    qi = pl.program_id(1)
    tq = ATT_BLOCK
    lane = lax.broadcasted_iota(jnp.int32, (tq, LANES), 1)
    _stack_heads(q_ref, qs_ref, [(lane // HEAD_DIM) == hh for hh in range(2)], LANES)
    acc_ref[...] = jnp.zeros_like(acc_ref)
    row = lax.broadcasted_iota(jnp.int32, (2 * tq, tq), 0) & (tq - 1)
    col = lax.broadcasted_iota(jnp.int32, (2 * tq, tq), 1)
    strict = col < row
    later = (lax.broadcasted_iota(jnp.int32, (tq, tq), 0) > lax.broadcasted_iota(jnp.int32, (tq, tq), 1)).astype(BF16)

    def scores(kj, slot):
        start = pl.multiple_of(kj * tq, tq)
        for p in range(N_PAIRS):
            s_ref[slot, p] = _dot_nt(qs_ref[p], k_ref[pl.ds(start, tq), p * LANES:(p + 1) * LANES])

    def weights(slot, runs, diagonal):
        new_runs = []
        for p in range(N_PAIRS):
            z = s_ref[slot, p]
            log_beta = jnp.minimum(z, 0.0) - jnp.log(1.0 + jnp.exp(-jnp.abs(z)))
            log_keep = log_beta - z
            if diagonal:
                log_keep = jnp.where(strict, log_keep, 0.0)
            suffix = _dot(log_keep.astype(BF16), later)
            att = jnp.exp(log_beta + suffix + runs[p])
            if diagonal:
                att = jnp.where(strict, att, 0.0)
            w_ref[slot, p] = att.astype(BF16)
            new_runs.append(runs[p] + jnp.sum(log_keep, axis=1, keepdims=True))
        return tuple(new_runs)

    def accumulate(kj, slot):
        start = pl.multiple_of(kj * tq, tq)
        for p in range(N_PAIRS):
            vh = _head_values(v_ref, start, p)
            w = w_ref[slot, p]
            acc_ref[p] += jnp.concatenate([_dot(w[:tq], vh[0]), _dot(w[tq:], vh[1])], axis=0)

    scores(qi, 0)
    scores(jnp.maximum(qi - 1, 0), 1)
    runs = weights(0, tuple(jnp.zeros((2 * tq, 1), F32) for _ in range(N_PAIRS)), True)

    def trip(i, runs):
        slot = i & 1
        accumulate(qi - i + 1, 1 - slot)
        runs = weights(slot, runs, False)
        scores(jnp.maximum(qi - i - 1, 0), 1 - slot)
        return runs

    lax.fori_loop(1, qi + 1, trip, runs)
    accumulate(0, qi & 1)
    for p in range(N_PAIRS):
        o_ref[:, p * LANES:(p + 1) * LANES] = (acc_ref[p, :tq] + acc_ref[p, tq:]).astype(o_ref.dtype)


def _sb_attention(q, k, v, *, batch, seq):
    tq = ATT_BLOCK
    nq = seq // tq
    return pl.pallas_call(
        _sb_body,
        grid=(batch, nq),
        in_specs=[
            pl.BlockSpec((tq, HALF), lambda b, i: (b * nq + i, 0)),
            pl.BlockSpec((seq, HALF), lambda b, i: (b, 0)),
            pl.BlockSpec((seq, 2 * HALF), lambda b, i: (b, 0)),
        ],
        out_specs=pl.BlockSpec((tq, HALF), lambda b, i: (b * nq + i, 0)),
        out_shape=jax.ShapeDtypeStruct((batch * seq, HALF), BF16),
        scratch_shapes=[pltpu.VMEM((N_PAIRS, 2 * tq, LANES), BF16), pltpu.VMEM((N_PAIRS, 2 * tq, LANES), F32),
                        pltpu.VMEM((2, N_PAIRS, 2 * tq, tq), F32), pltpu.VMEM((2, N_PAIRS, 2 * tq, tq), BF16)],
        compiler_params=_params("arbitrary", "arbitrary"),
        name="sb_attention",
    )(q, k, v)


def _mla_body(q_ref, k_ref, vt_ref, o_ref, qs_ref, acc_ref, s_ref, w_ref):
    qi = pl.program_id(1)
    tq = ATT_BLOCK
    scale = (MLA_NOPE + MLA_ROPE) ** -0.5
    lane2 = lax.broadcasted_iota(jnp.int32, (tq, PAIR), 1)
    masks = [((lane2 >= hh * MLA_NOPE) & (lane2 < (hh + 1) * MLA_NOPE))
             | ((lane2 >= LANES + hh * MLA_ROPE) & (lane2 < LANES + (hh + 1) * MLA_ROPE)) for hh in range(2)]
    _stack_heads(q_ref, qs_ref, masks, PAIR)
    acc_ref[...] = jnp.zeros_like(acc_ref)
    key = lax.broadcasted_iota(jnp.int32, (tq, 2 * tq), 0)
    qry = lax.broadcasted_iota(jnp.int32, (tq, 2 * tq), 1) & (tq - 1)
    causal = key <= qry

    def scores(kj, slot):
        start = pl.multiple_of(kj * tq, tq)
        for p in range(N_PAIRS):
            k = k_ref[pl.ds(start, tq), p * PAIR:(p + 1) * PAIR]
            s_ref[slot, p] = _dot_nt(k, qs_ref[p]) * scale

    def values(kj, slot, p):
        start = pl.multiple_of(kj * tq, tq)
        vt = [vt_ref[hh * HALF + p * LANES:hh * HALF + (p + 1) * LANES, pl.ds(start, tq)] for hh in range(2)]
        w = w_ref[slot, p]
        return jnp.concatenate([_dot(vt[0], w[:, :tq]), _dot(vt[1], w[:, tq:])], axis=1)

    def weights(slot, carry, diagonal, prev=None):
        pvs = [values(prev[0], prev[1], p) for p in range(N_PAIRS)] if prev is not None else None
        new = []
        for p in range(N_PAIRS):
            m, l = carry[p]
            s = s_ref[slot, p]
            if diagonal:
                s = jnp.where(causal, s, NEG_BIG)
            m_new = jnp.maximum(m, jnp.max(s, axis=0, keepdims=True))
            corr = jnp.exp(m - m_new)
            pexp = jnp.exp(s - m_new)
            l = l * corr + jnp.sum(pexp, axis=0, keepdims=True)
            w_ref[slot, p] = pexp.astype(BF16)
            if prev is not None:
                acc_ref[p] = (acc_ref[p] + pvs[p]) * corr
            new.append((m_new, l))
        return tuple(new)

    init = tuple((jnp.full((1, 2 * tq), NEG_BIG, F32), jnp.zeros((1, 2 * tq), F32)) for _ in range(N_PAIRS))
    scores(qi, 0)
    scores(0, 1)
    carry = weights(0, init, True)

    def trip(i, carry):
        slot = i & 1
        carry = weights(slot, carry, False, prev=(jnp.where(i == 1, qi, i - 2), 1 - slot))
        scores(jnp.minimum(i, jnp.maximum(qi - 1, 0)), 1 - slot)
        return carry

    carry = lax.fori_loop(1, qi + 1, trip, carry)
    last = (jnp.where(qi == 0, 0, qi - 1), qi & 1)
    for p in range(N_PAIRS):
        acc_ref[p] += values(last[0], last[1], p)
    for p in range(N_PAIRS):
        out = acc_ref[p] / carry[p][1]
        o_ref[:, p * LANES:(p + 1) * LANES] = (out[:, :tq] + out[:, tq:]).T.astype(o_ref.dtype)


def _mla_attention(q, k, v, *, batch, seq):
    tq = ATT_BLOCK
    nq = seq // tq
    return pl.pallas_call(
        _mla_body,
        grid=(batch, nq),
        in_specs=[
            pl.BlockSpec((tq, N_PAIRS * PAIR), lambda b, i: (b * nq + i, 0)),
            pl.BlockSpec((seq, N_PAIRS * PAIR), lambda b, i: (b, 0)),
            pl.BlockSpec((2 * HALF, seq), lambda b, i: (b, 0)),
        ],
        out_specs=pl.BlockSpec((tq, HALF), lambda b, i: (b * nq + i, 0)),
        out_shape=jax.ShapeDtypeStruct((batch * seq, HALF), BF16),
        scratch_shapes=[pltpu.VMEM((N_PAIRS, 2 * tq, PAIR), BF16), pltpu.VMEM((N_PAIRS, LANES, 2 * tq), F32),
                        pltpu.VMEM((2, N_PAIRS, tq, 2 * tq), F32), pltpu.VMEM((2, N_PAIRS, tq, 2 * tq), BF16)],
        compiler_params=_params("arbitrary", "arbitrary"),
        name="mla_attention",
    )(q, k, v)


def _row(v):
    return v.reshape(1, -1).astype(F32)


def _chain_layout(v):
    lane = jnp.arange(LANES)
    head = (2 * (lane % (LANES // 2)) + lane // (LANES // 2)) % N_HEADS
    return v.reshape(N_HEADS, HEAD_DIM).T[:, head].astype(F32)


def _rope_rotate_cols(w):
    half = MLA_ROPE // 2
    return jnp.concatenate([-w[..., half:], w[..., :half]], axis=-1)


def _layer0_mixer(x2, batch, seq, l0_w_in, rwkv_mix, rwkv_w0, rwkv_w2, rwkv_a0, rwkv_a2, rwkv_g2,
                  rwkv_k_k, rwkv_k_a, rwkv_r_k, rwkv_ln_g, rwkv_ln_b, ssm_conv_w, ssm_conv_b,
                  ssm_dt_bias, ssm_a_log, ssm_d, ssm_norm_g):
    w_r = l0_w_in[:, :RWKV_COLS].astype(BF16)
    zeros = jnp.zeros((RWKV_DECAY_LORA, HALF), F32)
    wl = jnp.concatenate([jnp.concatenate([rwkv_w2, zeros], axis=1),
                          jnp.concatenate([zeros, rwkv_a2], axis=1)], axis=0).astype(BF16)
    w0a0 = _row(jnp.concatenate([rwkv_w0, rwkv_a0]))
    w_s = l0_w_in[:, RWKV_COLS:]
    pad = jnp.zeros((D_MODEL, LANES - N_HEADS), F32)
    w_s = jnp.concatenate([w_s[:, HALF:HALF + SSM_XBC], w_s[:, :HALF], w_s[:, HALF + SSM_XBC:], pad], axis=1)
    *rkvwa, gate, ps = _rwkv_in(x2, w_r, _row(rwkv_mix), wl, w0a0, rwkv_g2.astype(BF16), w_s.astype(BF16), seq=seq)
    as_tiles = lambda t: t.reshape(seq, HEAD_DIM, LANES)
    y = _rwkv_scan(*[as_tiles(t) for t in rkvwa], _chain_layout(rwkv_k_k), _chain_layout(rwkv_k_a),
                   _chain_layout(rwkv_r_k.reshape(-1)), _chain_layout(rwkv_ln_g), _chain_layout(rwkv_ln_b))
    y_a = y.reshape(seq, batch * HALF)

    dt_raw = ps[:, SSM_XBC + HALF:SSM_XBC + HALF + N_HEADS]
    dtt = dt_raw.reshape(batch, seq, N_HEADS).transpose(0, 2, 1)
    a_neg = -jnp.exp(ssm_a_log.astype(F32))
    lane_pad = jnp.zeros((LANES - N_HEADS,), F32)
    expand = jnp.concatenate([jnp.repeat(jnp.eye(N_HEADS, dtype=F32), HEAD_DIM, axis=1),
                              jnp.zeros((LANES - N_HEADS, HALF), F32)], axis=0)
    y_b = _ssd(ps, dtt, ssm_conv_w.astype(F32), _row(ssm_conv_b),
               _row(jnp.concatenate([ssm_dt_bias, lane_pad])), ssm_dt_bias.reshape(N_HEADS, 1).astype(F32),
               _row(jnp.concatenate([a_neg, lane_pad])), a_neg.reshape(N_HEADS, 1),
               expand, _row(jnp.repeat(ssm_d, HEAD_DIM)), _row(ssm_norm_g), batch=batch, seq=seq)
    return y_a, gate, y_b


def _layer1_mixer(h2, positions, batch, seq, l1_w_in, mla_q_norm_g, mla_w_uq, mla_kv_norm_g, mla_w_ukv):
    off = L1_SB + MLA_Q_LORA + MLA_KV_LORA
    w_kpe = l1_w_in[:, off:off + MLA_ROPE]
    lane_zeros = jnp.zeros((D_MODEL, LANES - 2 * MLA_ROPE), F32)
    w_kpe_blk = jnp.concatenate([w_kpe, w_kpe, lane_zeros], axis=1)
    w_kpe_rot = _rope_rotate_cols(w_kpe)
    w_kpe_rot_blk = jnp.concatenate([w_kpe_rot, w_kpe_rot, lane_zeros], axis=1)
    w_in = jnp.concatenate([l1_w_in[:, :off], w_kpe_blk, w_kpe_rot_blk], axis=1).astype(BF16)

    inv_freq = 1.0 / (ROPE_THETA ** (jnp.arange(0, MLA_ROPE, 2, dtype=F32) / MLA_ROPE))
    invf = jnp.concatenate([jnp.tile(inv_freq, 4), jnp.zeros((LANES - 2 * MLA_ROPE,), F32)]).reshape(1, LANES)

    wq = mla_w_uq.reshape(MLA_Q_LORA, N_HEADS, MLA_NOPE + MLA_ROPE)
    wq_nope = wq[:, :, :MLA_NOPE].reshape(MLA_Q_LORA, N_PAIRS, 2 * MLA_NOPE)
    wq_pe = wq[:, :, MLA_NOPE:]
    lz = jnp.zeros((MLA_Q_LORA, N_PAIRS, LANES - 2 * MLA_ROPE), F32)
    wq_pe_blk = jnp.concatenate([wq_pe.reshape(MLA_Q_LORA, N_PAIRS, 2 * MLA_ROPE), lz], axis=2)
    wq_rot_blk = jnp.concatenate([_rope_rotate_cols(wq_pe).reshape(MLA_Q_LORA, N_PAIRS, 2 * MLA_ROPE), lz], axis=2)
    wuq = jnp.concatenate([jnp.concatenate([wq_nope, wq_pe_blk], axis=2).reshape(MLA_Q_LORA, N_PAIRS * PAIR),
                           wq_rot_blk.reshape(MLA_Q_LORA, N_PAIRS * LANES)], axis=1).astype(BF16)

    wkv = mla_w_ukv.reshape(MLA_KV_LORA, N_HEADS, MLA_NOPE + MLA_V)
    wukv = jnp.concatenate([wkv[:, :, :MLA_NOPE].reshape(MLA_KV_LORA, HALF),
                            wkv[:, :, MLA_NOPE:].reshape(MLA_KV_LORA, HALF)], axis=1).astype(BF16)

    qsb, ksb, vsb, q, k, v = _l1_in(h2, positions.reshape(batch * seq, 1), w_in, invf,
                                    _row(mla_q_norm_g), wuq, _row(mla_kv_norm_g), wukv, seq=seq)
    y_c = _sb_attention(qsb, ksb, vsb, batch=batch, seq=seq)
    y_d = _mla_attention(q, k, v, batch=batch, seq=seq)
    return y_c, y_d


def _ffn_layer(h2, seq, w_up, conv_w, conv_b, w_down, ln_g, ln_b):
    return _ffn(h2, w_up[:, :D_FF].astype(BF16), w_up[:, D_FF:].astype(BF16), conv_w.astype(F32),
                _row(conv_b), w_down.astype(BF16), _row(ln_g), _row(ln_b), seq=seq)


def kernel(x, positions, l0_w_in, rwkv_mix, rwkv_w0, rwkv_w2, rwkv_a0, rwkv_a2, rwkv_g2, rwkv_k_k, rwkv_k_a, rwkv_r_k, rwkv_ln_g, rwkv_ln_b, ssm_conv_w, ssm_conv_b, ssm_dt_bias, ssm_a_log, ssm_d, ssm_norm_g, l0_w_out, l0_ln1_g, l0_ln1_b, ffn0_w_up, ffn0_conv_w, ffn0_conv_b, ffn0_w_down, l0_ln2_g, l0_ln2_b, l1_w_in, mla_q_norm_g, mla_w_uq, mla_kv_norm_g, mla_w_ukv, l1_w_out, l1_ln1_g, l1_ln1_b, ffn1_w_up, ffn1_conv_w, ffn1_conv_b, ffn1_w_down, l1_ln2_g, l1_ln2_b):
    batch, seq, _ = x.shape
    assert batch * N_HEADS == LANES, "the RWKV scan maps batch*heads onto the lane axis"
    x2 = x.reshape(batch * seq, D_MODEL).astype(F32)

    y_a, gate, y_b = _layer0_mixer(x2, batch, seq, l0_w_in, rwkv_mix, rwkv_w0, rwkv_w2, rwkv_a0, rwkv_a2,
                                   rwkv_g2, rwkv_k_k, rwkv_k_a, rwkv_r_k, rwkv_ln_g, rwkv_ln_b, ssm_conv_w,
                                   ssm_conv_b, ssm_dt_bias, ssm_a_log, ssm_d, ssm_norm_g)
    h = _mix_out(y_a, gate, y_b, x2, l0_w_out.astype(BF16), _row(l0_ln1_g), _row(l0_ln1_b), seq=seq)
    h = _ffn_layer(h, seq, ffn0_w_up, ffn0_conv_w, ffn0_conv_b, ffn0_w_down, l0_ln2_g, l0_ln2_b)

    y_c, y_d = _layer1_mixer(h, positions, batch, seq, l1_w_in, mla_q_norm_g, mla_w_uq, mla_kv_norm_g, mla_w_ukv)
    h = _mix_out(y_c, None, y_d, h, l1_w_out.astype(BF16), _row(l1_ln1_g), _row(l1_ln1_b), seq=seq)
    h = _ffn_layer(h, seq, ffn1_w_up, ffn1_conv_w, ffn1_conv_b, ffn1_w_down, l1_ln2_g, l1_ln2_b)
    return h.reshape(batch, seq, D_MODEL).astype(x.dtype)
```
